```python
import jax, jax.numpy as jnp
from jax import lax
import numpy as np

D_MODEL = 1024
BATCH = 4
SEQ = 8192
DEPTH = 1

GRID_W = 64
MEM_TOKENS = 256
MLSTM_HEADS = 4
MLSTM_QK_DIM = 64
MLSTM_V_DIM = 128
MLSTM_CHUNK = 64
MLSTM_CONV = 5
NA_HEADS = 8
NA_HEAD_DIM = 32
NA_WIN_ROWS = 8
NA_WIN_COLS = 16
MEM_HEADS = 4
MEM_HEAD_DIM = 64
N_BRANCH = 3
D_FF = 4 * D_MODEL
EPS = 1e-6

MLSTM_QK_W = MLSTM_HEADS * MLSTM_QK_DIM
MLSTM_V_W = MLSTM_HEADS * MLSTM_V_DIM
NA_W = NA_HEADS * NA_HEAD_DIM
MEM_W = MEM_HEADS * MEM_HEAD_DIM
IN_SPLITS = (2 * MLSTM_QK_W, MLSTM_V_W, MLSTM_V_W, 2 * MLSTM_HEADS, 2 * MLSTM_HEADS,
             3 * NA_W, MEM_W, N_BRANCH * D_MODEL)
D_IN = sum(IN_SPLITS)

kernel_name = 'hybrid_mlstm_natten_memxattn_block'


def rmsnorm(x, g):
    xf = x.astype(jnp.float32)
    y = xf * lax.rsqrt(jnp.mean(xf * xf, axis=-1, keepdims=True) + EPS)
    return (y * g.astype(jnp.float32)).astype(x.dtype)


def short_conv(x, w, b):
    c = x.shape[-1]
    y = lax.conv_general_dilated(
        x, w[:, None, :].astype(x.dtype), window_strides=(1,),
        padding=[(MLSTM_CONV // 2, MLSTM_CONV // 2)],
        dimension_numbers=('NWC', 'WIO', 'NWC'), feature_group_count=c)
    return y + b.astype(x.dtype)


def _mlstm_chunk_step(carry, inp):
    c_st, n_st, m_st = carry
    q, k, v, ig, lf = inp
    L = q.shape[-2]
    tril = jnp.tril(jnp.ones((L, L), dtype=bool))
    b = jnp.cumsum(lf, axis=-1)
    d_log = jnp.where(tril, b[..., :, None] - b[..., None, :] + ig[..., None, :], -jnp.inf)
    inter = b + m_st[..., None]
    m_t = jnp.maximum(inter, jnp.max(d_log, axis=-1))
    s = jnp.einsum('bhtd,bhsd->bhts', q, k) * jnp.exp(d_log - m_t[..., None])
    w_inter = jnp.exp(inter - m_t)
    num = (jnp.einsum('bhts,bhsv->bhtv', s, v)
           + w_inter[..., None] * jnp.einsum('bhvd,bhtd->bhtv', c_st, q))
    den = jnp.sum(s, axis=-1) + w_inter * jnp.einsum('bhd,bhtd->bht', n_st, q)
    h = num / jnp.maximum(jnp.abs(den), jnp.exp(-m_t))[..., None]
    b_last = b[..., -1]
    w_log = b_last[..., None] - b + ig
    m_new = jnp.maximum(b_last + m_st, jnp.max(w_log, axis=-1))
    decay = jnp.exp(b_last + m_st - m_new)
    ws = jnp.exp(w_log - m_new[..., None])
    c_new = decay[..., None, None] * c_st + jnp.einsum('bhs,bhsv,bhsd->bhvd', ws, v, k)
    n_new = decay[..., None] * n_st + jnp.einsum('bhs,bhsd->bhd', ws, k)
    return (c_new, n_new, m_new), h


def mlstm_chunkwise(q, k, v, ig, lf):
    bsz, s, nh, dk = q.shape
    dv = v.shape[-1]
    nc = s // MLSTM_CHUNK

    def chunks(a):
        a = a.reshape((bsz, nc, MLSTM_CHUNK, nh) + a.shape[3:])
        return jnp.swapaxes(jnp.moveaxis(a, 1, 0), 2, 3)

    init = (jnp.zeros((bsz, nh, dv, dk), jnp.float32),
            jnp.zeros((bsz, nh, dk), jnp.float32),
            jnp.full((bsz, nh), -jnp.inf, jnp.float32))
    _, hs = lax.scan(_mlstm_chunk_step, init,
                     (chunks(q), chunks(k), chunks(v), chunks(ig), chunks(lf)))
    return jnp.swapaxes(jnp.moveaxis(hs, 0, 1), 2, 3).reshape(bsz, s, nh, dv)


def neighbourhood_attention(q, k, v, rpb):
    bsz, s, nh, dh = q.shape
    rows = s // GRID_W
    win_r = min(NA_WIN_ROWS, rows)
    grid = lambda a: a.reshape(bsz, rows, GRID_W, nh, dh)
    qg, kg, vg = grid(q), grid(k), grid(v)
    row_start = jnp.clip(jnp.arange(rows) - win_r // 2, 0, rows - win_r)
    cols = jnp.arange(GRID_W)
    col_idx = (jnp.clip(cols - NA_WIN_COLS // 2, 0, GRID_W - NA_WIN_COLS)[:, None]
               + jnp.arange(NA_WIN_COLS))
    col_off = col_idx - cols[:, None] + (NA_WIN_COLS - 1)

    def one_row(r):
        rs = row_start[r]
        q_row = lax.dynamic_index_in_dim(qg, r, axis=1, keepdims=False)
        k_win = lax.dynamic_slice_in_dim(kg, rs, win_r, axis=1)[:, :, col_idx]
        v_win = lax.dynamic_slice_in_dim(vg, rs, win_r, axis=1)[:, :, col_idx]
        row_off = rs + jnp.arange(win_r) - r + (NA_WIN_ROWS - 1)
        bias = rpb[:, row_off[None, :, None], col_off[:, None, :]]
        sc = (jnp.einsum('bchd,bicjhd->bhcij', q_row, k_win).astype(jnp.float32)
              + bias.astype(jnp.float32)[None])
        p = jax.nn.softmax(sc.reshape(bsz, nh, GRID_W, win_r * NA_WIN_COLS), axis=-1)
        p = p.reshape(sc.shape).astype(v.dtype)
        return jnp.einsum('bhcij,bicjhd->bchd', p, v_win)

    out = lax.map(one_row, jnp.arange(rows))
    return jnp.moveaxis(out, 0, 1).reshape(bsz, s, nh * dh)


def memory_cross_attention(q, mem_n, w_mem_kv, g_mem_q, g_mem_k):
    bsz, s, _ = q.shape
    m = mem_n.shape[1]
    k, v = jnp.split(mem_n @ w_mem_kv, 2, axis=-1)
    q = rmsnorm(q.reshape(bsz, s, MEM_HEADS, MEM_HEAD_DIM), g_mem_q) * (MEM_HEAD_DIM ** -0.5)
    k = rmsnorm(k.reshape(bsz, m, MEM_HEADS, MEM_HEAD_DIM), g_mem_k)
    v = v.reshape(bsz, m, MEM_HEADS, MEM_HEAD_DIM)
    sc = jnp.einsum('bshd,bmhd->bhsm', q, k).astype(jnp.float32)
    p = jax.nn.softmax(sc, axis=-1).astype(v.dtype)
    return jnp.einsum('bhsm,bmhd->bshd', p, v).reshape(bsz, s, MEM_W)


def hybrid_layer(x, mem, g_mix, w_in, conv_w, conv_b, b_igate, b_fgate, g_mlstm, w_proj_mlstm,
                 g_na_q, g_na_k, rpb, w_proj_na, g_mem, w_mem_kv, g_mem_q, g_mem_k,
                 w_proj_mem, w_out, g_ffn, w_up, w_down):
    bsz, s, _ = x.shape
    h = rmsnorm(x, g_mix)
    z = h @ w_in
    split_idx = np.cumsum(np.array(IN_SPLITS))[:-1]
    qk_m, v_m, o_m, i_pre, f_pre, qkv_na, q_mem, gate_pre = jnp.split(z, split_idx, axis=-1)

    qk_m = jax.nn.silu(short_conv(qk_m, conv_w, conv_b))
    q_m, k_m = jnp.split(qk_m, 2, axis=-1)
    q_m = q_m.reshape(bsz, s, MLSTM_HEADS, MLSTM_QK_DIM).astype(jnp.float32) * (MLSTM_QK_DIM ** -0.5)
    k_m = k_m.reshape(bsz, s, MLSTM_HEADS, MLSTM_QK_DIM).astype(jnp.float32)
    vv = v_m.reshape(bsz, s, MLSTM_HEADS, MLSTM_V_DIM).astype(jnp.float32)
    i_log = i_pre.reshape(bsz, s, 2, MLSTM_HEADS).astype(jnp.float32) + b_igate.astype(jnp.float32)
    f_log = jax.nn.log_sigmoid(f_pre.reshape(bsz, s, 2, MLSTM_HEADS).astype(jnp.float32)
                               + b_fgate.astype(jnp.float32))
    flip = lambda a: jnp.flip(a, axis=1)
    h_fwd = mlstm_chunkwise(q_m, k_m, vv, i_log[:, :, 0], f_log[:, :, 0])
    h_bwd = flip(mlstm_chunkwise(flip(q_m), flip(k_m), flip(vv),
                                 flip(i_log[:, :, 1]), flip(f_log[:, :, 1])))
    h_m = rmsnorm(h_fwd + h_bwd, g_mlstm.reshape(MLSTM_HEADS, MLSTM_V_DIM))
    h_m = (h_m.reshape(bsz, s, MLSTM_V_W) * jax.nn.sigmoid(o_m.astype(jnp.float32))).astype(x.dtype)

    q_na, k_na, v_na = jnp.split(qkv_na.reshape(bsz, s, 3, NA_HEADS, NA_HEAD_DIM), 3, axis=2)
    q_na = rmsnorm(q_na[:, :, 0], g_na_q) * (NA_HEAD_DIM ** -0.5)
    k_na = rmsnorm(k_na[:, :, 0], g_na_k)
    h_na = neighbourhood_attention(q_na, k_na, v_na[:, :, 0], rpb)

    h_mem = memory_cross_attention(q_mem, rmsnorm(mem, g_mem), w_mem_kv, g_mem_q, g_mem_k)

    gates = jax.nn.sigmoid(gate_pre).reshape(bsz, s, N_BRANCH, D_MODEL)
    y = (gates[:, :, 0] * (h_m @ w_proj_mlstm)
         + gates[:, :, 1] * (h_na @ w_proj_na)
         + gates[:, :, 2] * (h_mem @ w_proj_mem))
    x = x + y @ w_out

    u = rmsnorm(x, g_ffn) @ w_up
    return x + jnp.square(jax.nn.relu(u)) @ w_down


def setup_inputs(seed: int = 0) -> dict:
    key = jax.random.key(seed)
    ks = jax.random.split(key, 24)
    nrm = lambda k, shape, scale: jax.random.normal(k, shape, jnp.float32) * scale
    gain = lambda k, shape: 1.0 + 0.1 * jax.random.normal(k, shape, jnp.float32)
    L = DEPTH
    f_bias = (jnp.linspace(3.0, 6.0, MLSTM_HEADS, dtype=jnp.float32)[None, None, :]
              + nrm(ks[8], (L, 2, MLSTM_HEADS), 0.1))
    return {
        'x': nrm(ks[0], (BATCH, SEQ, D_MODEL), 1.0),
        'mem': nrm(ks[1], (BATCH, MEM_TOKENS, D_MODEL), 1.0),
        'g_mix': gain(ks[2], (L, D_MODEL)),
        'w_in': nrm(ks[3], (L, D_MODEL, D_IN), D_MODEL ** -0.5),
        'conv_w': nrm(ks[4], (L, MLSTM_CONV, 2 * MLSTM_QK_W), MLSTM_CONV ** -0.5),
        'conv_b': nrm(ks[5], (L, 2 * MLSTM_QK_W), 0.02),
        'b_igate': nrm(ks[6], (L, 2, MLSTM_HEADS), 0.1),
        'b_fgate': f_bias,
        'g_mlstm': gain(ks[7], (L, MLSTM_V_W)),
        'w_proj_mlstm': nrm(ks[9], (L, MLSTM_V_W, D_MODEL), MLSTM_V_W ** -0.5),
        'g_na_q': gain(ks[10], (L, NA_HEAD_DIM)),
        'g_na_k': gain(ks[11], (L, NA_HEAD_DIM)),
        'rpb': nrm(ks[12], (L, NA_HEADS, 2 * NA_WIN_ROWS - 1, 2 * NA_WIN_COLS - 1), 0.1),
        'w_proj_na': nrm(ks[13], (L, NA_W, D_MODEL), NA_W ** -0.5),
        'g_mem': gain(ks[14], (L, D_MODEL)),
        'w_mem_kv': nrm(ks[15], (L, D_MODEL, 2 * MEM_W), D_MODEL ** -0.5),
        'g_mem_q': gain(ks[16], (L, MEM_HEAD_DIM)),
        'g_mem_k': gain(ks[17], (L, MEM_HEAD_DIM)),
        'w_proj_mem': nrm(ks[18], (L, MEM_W, D_MODEL), MEM_W ** -0.5),
        'w_out': nrm(ks[19], (L, D_MODEL, D_MODEL), D_MODEL ** -0.5),
        'g_ffn': gain(ks[20], (L, D_MODEL)),
        'w_up': nrm(ks[21], (L, D_MODEL, D_FF), D_MODEL ** -0.5),
        'w_down': nrm(ks[22], (L, D_FF, D_MODEL), D_FF ** -0.5),
    }


def reference(x, mem, g_mix, w_in, conv_w, conv_b, b_igate, b_fgate, g_mlstm, w_proj_mlstm,
              g_na_q, g_na_k, rpb, w_proj_na, g_mem, w_mem_kv, g_mem_q, g_mem_k,
              w_proj_mem, w_out, g_ffn, w_up, w_down):
    for l in range(DEPTH):
        x = hybrid_layer(x, mem, g_mix[l], w_in[l], conv_w[l], conv_b[l], b_igate[l], b_fgate[l],
                         g_mlstm[l], w_proj_mlstm[l], g_na_q[l], g_na_k[l], rpb[l], w_proj_na[l],
                         g_mem[l], w_mem_kv[l], g_mem_q[l], g_mem_k[l], w_proj_mem[l], w_out[l],
                         g_ffn[l], w_up[l], w_down[l])
    return x
```

```python
import functools

import numpy as np
import jax
import jax.numpy as jnp
from jax import lax
from jax.experimental import pallas as pl
from jax.experimental.pallas import tpu as pltpu

GRID_W = 64
MLSTM_HEADS = 4
MLSTM_QK_DIM = 64
MLSTM_V_DIM = 128
MLSTM_CONV = 5
NA_HEADS = 8
NA_HEAD_DIM = 32
NA_WIN_ROWS = 8
NA_WIN_COLS = 16
MEM_HEADS = 4
MEM_HEAD_DIM = 64
N_BRANCH = 3
EPS = 1e-6

QK_W = MLSTM_HEADS * MLSTM_QK_DIM
V_W = MLSTM_HEADS * MLSTM_V_DIM
NA_W = NA_HEADS * NA_HEAD_DIM
MEM_W = MEM_HEADS * MEM_HEAD_DIM
N_GATES = 4 * MLSTM_HEADS
LANES = 128
GATE_PAD = LANES
NEG_BIG = -1e30

MLSTM_CHUNK = 128
VMEM_LIMIT = 48 * 1024 * 1024

BF16 = jnp.bfloat16
F32 = jnp.float32


def _dot(a, b):
    return jnp.dot(a, b, preferred_element_type=F32)


def _dot_nt(a, b):
    return lax.dot_general(a, b, (((1,), (1,)), ((), ())), preferred_element_type=F32)


def _dot_tn(a, b):
    return lax.dot_general(a, b, (((0,), (0,)), ((), ())), preferred_element_type=F32)


def _split_dot(a, m_bf16):
    hi = a.astype(BF16)
    lo = (a - hi.astype(F32)).astype(BF16)
    return _dot(hi, m_bf16) + _dot(lo, m_bf16)


def _rms_rows(x, g):
    ms = jnp.mean(x * x, axis=-1, keepdims=True)
    return x * lax.rsqrt(ms + EPS) * g


def _group_mean_matrix(width, group):
    idx = np.arange(width) // group
    return jnp.asarray((idx[:, None] == idx[None, :]).astype(np.float32) / group, dtype=BF16)


def _const_spec(shape):
    nd = len(shape)
    return pl.BlockSpec(shape, lambda *_: (0,) * nd, pipeline_mode=pl.Buffered(1))


def _proj_in_kernel(x_ref, g_ref, w_ref, gm_ref, gq_ref, gk_ref,
                    qk_ref, v_ref, if_ref, qn_ref, kn_ref, vn_ref):
    h = _rms_rows(x_ref[...], g_ref[...]).astype(BF16)
    o = 0
    qk_ref[...] = _dot(h, w_ref[:, o:o + 2 * QK_W]); o += 2 * QK_W
    v_ref[...] = _dot(h, w_ref[:, o:o + V_W]); o += V_W
    if_ref[...] = _dot(h, w_ref[:, o:o + GATE_PAD]); o += GATE_PAD
    q = _dot(h, w_ref[:, o:o + NA_W]); o += NA_W
    k = _dot(h, w_ref[:, o:o + NA_W]); o += NA_W
    vn_ref[...] = _dot(h, w_ref[:, o:o + NA_W]).astype(BF16)
    gm = gm_ref[...]
    qn = q * lax.rsqrt(_split_dot(q * q, gm) + EPS) * gq_ref[...]
    qn_ref[...] = (qn * (NA_HEAD_DIM ** -0.5)).astype(BF16)
    kn_ref[...] = (k * lax.rsqrt(_split_dot(k * k, gm) + EPS) * gk_ref[...]).astype(BF16)


def _proj_in(x2, g_mix, w1, gq, gk, tm):
    n, d = x2.shape
    w_cols = w1.shape[1]
    row = lambda width: pl.BlockSpec((tm, width), lambda i: (i, 0))
    return pl.pallas_call(
        _proj_in_kernel,
        grid=(n // tm,),
        in_specs=[row(d), _const_spec((1, d)), _const_spec((d, w_cols)),
                  _const_spec((NA_W, NA_W)), _const_spec((1, NA_W)), _const_spec((1, NA_W))],
        out_specs=[row(2 * QK_W), row(V_W), row(GATE_PAD), row(NA_W), row(NA_W), row(NA_W)],
        out_shape=[jax.ShapeDtypeStruct((n, 2 * QK_W), F32),
                   jax.ShapeDtypeStruct((n, V_W), F32),
                   jax.ShapeDtypeStruct((n, GATE_PAD), F32),
                   jax.ShapeDtypeStruct((n, NA_W), BF16),
                   jax.ShapeDtypeStruct((n, NA_W), BF16),
                   jax.ShapeDtypeStruct((n, NA_W), BF16)],
        compiler_params=pltpu.CompilerParams(dimension_semantics=("parallel",),
                                             vmem_limit_bytes=VMEM_LIMIT),
        name="proj_in",
    )(x2, g_mix, w1, _group_mean_matrix(NA_W, NA_HEAD_DIM), gq, gk)


def _mem_kv_kernel(mem_ref, g_ref, w_ref, gm_ref, gk_ref, k_ref, v_ref):
    h = _rms_rows(mem_ref[...], g_ref[...]).astype(BF16)
    k = _dot(h, w_ref[:, :MEM_W])
    v_ref[...] = _dot(h, w_ref[:, MEM_W:]).astype(BF16)
    k_ref[...] = (k * lax.rsqrt(_split_dot(k * k, gm_ref[...]) + EPS) * gk_ref[...]).astype(BF16)


def _mem_kv(mem, g_mem, w_kv, gk):
    b, m, d = mem.shape
    blk = lambda width: pl.BlockSpec((None, m, width), lambda i: (i, 0, 0))
    return pl.pallas_call(
        _mem_kv_kernel,
        grid=(b,),
        in_specs=[blk(d), _const_spec((1, d)), _const_spec((d, 2 * MEM_W)),
                  _const_spec((MEM_W, MEM_W)), _const_spec((1, MEM_W))],
        out_specs=[blk(MEM_W), blk(MEM_W)],
        out_shape=[jax.ShapeDtypeStruct((b, m, MEM_W), BF16)] * 2,
        compiler_params=pltpu.CompilerParams(dimension_semantics=("parallel",),
                                             vmem_limit_bytes=VMEM_LIMIT),
        name="mem_kv",
    )(mem, g_mem, w_kv, _group_mean_matrix(MEM_W, MEM_HEAD_DIM), gk)


def _log_sigmoid(x):
    return jnp.minimum(x, 0.0) - jnp.log1p(jnp.exp(-jnp.abs(x)))


def _conv_silu(main, prev, nxt, has_prev, has_next, w, b):
    L = main.shape[0]
    rows = lax.broadcasted_iota(jnp.int32, main.shape, 0)
    prev = jnp.where(has_prev, prev, 0.0)
    nxt = jnp.where(has_next, nxt, 0.0)
    half = MLSTM_CONV // 2
    acc = jnp.zeros_like(main) + b
    for j in range(MLSTM_CONV):
        d = j - half
        if d == 0:
            tap = main
        else:
            tap = pltpu.roll(main, (-d) % L, 0)
            if d < 0:
                for r in range(-d):
                    tap = jnp.where(rows == r, prev[8 + d + r:8 + d + r + 1, :], tap)
            else:
                for r in range(d):
                    tap = jnp.where(rows == L - d + r, nxt[r:r + 1, :], tap)
        acc = acc + tap * w[j:j + 1, :]
    return acc * jax.nn.sigmoid(acc)


def _mlstm_direction(reverse, d, qk, v, gates, st_ref, m_ref, out_ref):
    L = qk.shape[0]
    r_i = lax.broadcasted_iota(jnp.int32, (L, L), 0)
    c_i = lax.broadcasted_iota(jnp.int32, (L, L), 1)
    mask = (c_i >= r_i) if reverse else (c_i <= r_i)
    tri = jnp.where(mask, 1.0, 0.0).astype(BF16)
    lf = _log_sigmoid(gates)
    l1 = lf.astype(BF16)
    r1 = lf - l1.astype(F32)
    l2 = r1.astype(BF16)
    l3 = (r1 - l2.astype(F32)).astype(BF16)
    b_cols = _dot(tri, l1) + _dot(tri, l2) + _dot(tri, l3)
    b_rows = b_cols.T
    g_rows = gates.T
    lane = lax.broadcasted_iota(jnp.int32, (L, LANES), 1)
    ones_col = jnp.where(lane == 0, 1.0, 0.0).astype(BF16)
    last = 0 if reverse else L - 1
    for hd in range(MLSTM_HEADS):
        ci = d * MLSTM_HEADS + hd
        fi = 2 * MLSTM_HEADS + ci
        chain = ci
        pair = hd // 2
        in_head = (lane // MLSTM_QK_DIM) == (hd % 2)
        q_h = jnp.where(in_head, qk[:, pair * LANES:(pair + 1) * LANES], 0.0) * (MLSTM_QK_DIM ** -0.5)
        k_h = jnp.where(in_head, qk[:, QK_W + pair * LANES:QK_W + (pair + 1) * LANES], 0.0)
        v_ext = jnp.concatenate([v[:, hd * LANES:(hd + 1) * LANES].astype(BF16), ones_col], axis=1)
        b_col = b_cols[:, fi:fi + 1]
        b_row = b_rows[fi:fi + 1, :]
        ig_row = g_rows[ci:ci + 1, :]
        ig_col = gates[:, ci:ci + 1]
        m_st = m_ref[chain, 0:1, 0:1]
        st = st_ref[chain]
        d_log = jnp.where(mask, b_col - b_row + ig_row, -jnp.inf)
        inter = b_col + m_st
        m_t = jnp.maximum(inter, jnp.max(d_log, axis=1, keepdims=True))
        q_b = q_h.astype(BF16)
        s_mat = _dot_nt(q_b, k_h.astype(BF16)) * jnp.exp(d_log - m_t)
        w_inter = jnp.exp(inter - m_t)
        num_ext = _dot(s_mat.astype(BF16), v_ext) + w_inter * _dot(q_b, st.astype(BF16))
        den = num_ext[:, LANES:LANES + 1]
        hval = num_ext[:, :LANES] / jnp.maximum(jnp.abs(den), jnp.exp(-m_t))
        out_ref[:, hd * LANES:(hd + 1) * LANES] = hval
        b_last = b_col[last:last + 1, :]
        w_log = b_last - b_col + ig_col
        m_new = jnp.maximum(b_last + m_st, jnp.max(w_log, axis=0, keepdims=True))
        decay = jnp.exp(b_last + m_st - m_new)
        ws = jnp.exp(w_log - m_new)
        st_ref[chain] = decay * st + _dot_tn((k_h * ws).astype(BF16), v_ext)
        m_ref[chain] = jnp.broadcast_to(m_new, m_ref.shape[1:])


def _mlstm_kernel(qkf_ref, qkf_prev_ref, qkf_next_ref, qkb_ref, qkb_prev_ref, qkb_next_ref,
                  vf_ref, vb_ref, iff_ref, ifb_ref, cw_ref, cb_ref, gb_ref,
                  hf_ref, hb_ref, st_ref, m_ref):
    c = pl.program_id(1)
    nc = pl.num_programs(1)

    @pl.when(c == 0)
    def _():
        st_ref[...] = jnp.zeros_like(st_ref)
        m_ref[...] = jnp.full(m_ref.shape, -jnp.inf, F32)

    cw = cw_ref[...]
    cb = cb_ref[...]
    gb = gb_ref[...]
    qk_f = _conv_silu(qkf_ref[...], qkf_prev_ref[...], qkf_next_ref[...], c > 0, c < nc - 1, cw, cb)
    _mlstm_direction(False, 0, qk_f, vf_ref[...], iff_ref[...] + gb, st_ref, m_ref, hf_ref)
    cr = nc - 1 - c
    qk_b = _conv_silu(qkb_ref[...], qkb_prev_ref[...], qkb_next_ref[...], cr > 0, cr < nc - 1, cw, cb)
    _mlstm_direction(True, 1, qk_b, vb_ref[...], ifb_ref[...] + gb, st_ref, m_ref, hb_ref)


def _mlstm(qk3, v3, if3, conv_w, conv_b, gate_b, L):
    b, s, _ = qk3.shape
    nc = s // L
    hb = L // 8
    nhb = s // 8
    fwd = lambda bi, c: c
    bwd = lambda bi, c: nc - 1 - c

    def main(width, pos):
        return pl.BlockSpec((None, L, width), lambda bi, c: (bi, pos(bi, c), 0))

    def prev(pos):
        return pl.BlockSpec((None, 8, 2 * QK_W),
                            lambda bi, c: (bi, jnp.maximum(pos(bi, c) * hb - 1, 0), 0))

    def nxt(pos):
        return pl.BlockSpec((None, 8, 2 * QK_W),
                            lambda bi, c: (bi, jnp.minimum((pos(bi, c) + 1) * hb, nhb - 1), 0))

    n_chain = 2 * MLSTM_HEADS
    return pl.pallas_call(
        _mlstm_kernel,
        grid=(b, nc),
        in_specs=[main(2 * QK_W, fwd), prev(fwd), nxt(fwd),
                  main(2 * QK_W, bwd), prev(bwd), nxt(bwd),
                  main(V_W, fwd), main(V_W, bwd),
                  main(GATE_PAD, fwd), main(GATE_PAD, bwd),
                  _const_spec((MLSTM_CONV, 2 * QK_W)), _const_spec((1, 2 * QK_W)),
                  _const_spec((1, GATE_PAD))],
        out_specs=[main(V_W, fwd), main(V_W, bwd)],
        out_shape=[jax.ShapeDtypeStruct((b, s, V_W), F32)] * 2,
        scratch_shapes=[pltpu.VMEM((n_chain, LANES, 2 * LANES), F32),
                        pltpu.VMEM((n_chain, 8, LANES), F32)],
        compiler_params=pltpu.CompilerParams(dimension_semantics=("parallel", "arbitrary"),
                                             vmem_limit_bytes=VMEM_LIMIT),
        name="mlstm",
    )(qk3, qk3, qk3, qk3, qk3, qk3, v3, v3, if3, if3, conv_w, conv_b, gate_b)


def _stack_heads(q, n_heads, head_dim):
    lane = lax.broadcasted_iota(jnp.int32, q.shape, 1)
    zero = jnp.zeros_like(q)
    return jnp.concatenate([jnp.where(lane // head_dim == h, q, zero) for h in range(n_heads)], axis=0)


def _unstack_heads(o, n_heads, head_dim):
    t = o.shape[0] // n_heads
    lane = lax.broadcasted_iota(jnp.int32, (t, o.shape[1]), 1)
    acc = jnp.zeros((t, o.shape[1]), o.dtype)
    for h in range(n_heads):
        acc = jnp.where(lane // head_dim == h, o[h * t:(h + 1) * t, :], acc)
    return acc


def _masked_attention(q_stacked, k, v, bias):
    sc = _dot_nt(q_stacked, k)
    if bias is not None:
        sc = sc + bias
    p = jnp.exp(sc - jnp.max(sc, axis=-1, keepdims=True))
    o = _dot(p.astype(BF16), v)
    return o / jnp.sum(p, axis=-1, keepdims=True)


def _natten_kernel(q_ref, k_ref, v_ref, bias_ref, o_ref, *, rows):
    r = pl.program_id(1)
    rs = jnp.clip(r - NA_WIN_ROWS // 2, 0, rows - NA_WIN_ROWS)
    start = pl.multiple_of(rs * GRID_W, GRID_W)
    k_win = k_ref[pl.ds(start, NA_WIN_ROWS * GRID_W), :]
    v_win = v_ref[pl.ds(start, NA_WIN_ROWS * GRID_W), :]
    o = _masked_attention(_stack_heads(q_ref[...], NA_HEADS, NA_HEAD_DIM), k_win, v_win, bias_ref[...])
    o_ref[...] = _unstack_heads(o, NA_HEADS, NA_HEAD_DIM).astype(o_ref.dtype)


def _na_bias_table(rpb):
    c = np.arange(GRID_W)
    kc = np.arange(GRID_W)
    cs = np.clip(c - NA_WIN_COLS // 2, 0, GRID_W - NA_WIN_COLS)
    valid = (kc[None, :] >= cs[:, None]) & (kc[None, :] < cs[:, None] + NA_WIN_COLS)
    col_off = np.clip(kc[None, :] - c[:, None] + NA_WIN_COLS - 1, 0, 2 * NA_WIN_COLS - 2)
    var = np.arange(NA_WIN_ROWS)
    i = np.arange(NA_WIN_ROWS)
    row_off = i[None, :] - var[:, None] + NA_WIN_ROWS - 1
    tab = rpb.astype(F32)[:, row_off[:, None, :, None], col_off[None, :, None, :]]
    tab = jnp.where(valid[None, None, :, None, :], tab, NEG_BIG)
    tab = jnp.transpose(tab, (1, 0, 2, 3, 4))
    return tab.reshape(NA_WIN_ROWS, NA_HEADS * GRID_W, NA_WIN_ROWS * GRID_W)


def _natten(qn, kn, vn, bias_tab):
    b, s, _ = qn.shape
    rows = s // GRID_W
    hc = NA_HEADS * GRID_W
    wk = NA_WIN_ROWS * GRID_W
    full = pl.BlockSpec((None, s, NA_W), lambda bi, r: (bi, 0, 0))

    def variant(bi, r):
        return (r - jnp.clip(r - NA_WIN_ROWS // 2, 0, rows - NA_WIN_ROWS), 0, 0)

    return pl.pallas_call(
        functools.partial(_natten_kernel, rows=rows),
        grid=(b, rows),
        in_specs=[pl.BlockSpec((None, GRID_W, NA_W), lambda bi, r: (bi, r, 0)), full, full,
                  pl.BlockSpec((None, hc, wk), variant)],
        out_specs=pl.BlockSpec((None, GRID_W, NA_W), lambda bi, r: (bi, r, 0)),
        out_shape=jax.ShapeDtypeStruct((b, s, NA_W), BF16),
        compiler_params=pltpu.CompilerParams(dimension_semantics=("parallel", "arbitrary"),
                                             vmem_limit_bytes=VMEM_LIMIT),
        name="natten",
    )(qn, kn, vn, bias_tab)


def _merge_kernel(x_ref, hf_ref, hb_ref, hna_ref, km_ref, vm_ref,
                  g_ref, w2_ref, gml_ref, gmq_ref, gm64_ref,
                  wpm_ref, wpn_ref, wpx_ref, wout_ref, o_ref):
    x = x_ref[...]
    d = x.shape[1]
    h = _rms_rows(x, g_ref[...]).astype(BF16)
    o_pre = _dot(h, w2_ref[:, :V_W])
    q_mem = _dot(h, w2_ref[:, V_W:V_W + MEM_W])
    g0 = V_W + MEM_W

    hm = hf_ref[...] + hb_ref[...]
    gml = gml_ref[...]
    parts = []
    for hd in range(MLSTM_HEADS):
        sl = slice(hd * MLSTM_V_DIM, (hd + 1) * MLSTM_V_DIM)
        parts.append(_rms_rows(hm[:, sl], gml[:, sl]))
    hm = (jnp.concatenate(parts, axis=1) * jax.nn.sigmoid(o_pre)).astype(BF16)
    y = jax.nn.sigmoid(_dot(h, w2_ref[:, g0:g0 + d])) * _dot(hm, wpm_ref[...])

    y = y + jax.nn.sigmoid(_dot(h, w2_ref[:, g0 + d:g0 + 2 * d])) * _dot(hna_ref[...], wpn_ref[...])

    qn = q_mem * lax.rsqrt(_split_dot(q_mem * q_mem, gm64_ref[...]) + EPS) * gmq_ref[...]
    qn = (qn * (MEM_HEAD_DIM ** -0.5)).astype(BF16)
    att = _masked_attention(_stack_heads(qn, MEM_HEADS, MEM_HEAD_DIM), km_ref[...], vm_ref[...], None)
    h_mem = _unstack_heads(att, MEM_HEADS, MEM_HEAD_DIM).astype(BF16)
    y = y + jax.nn.sigmoid(_dot(h, w2_ref[:, g0 + 2 * d:g0 + 3 * d])) * _dot(h_mem, wpx_ref[...])

    o_ref[...] = x + _dot(y.astype(BF16), wout_ref[...])


def _merge(x2, hf, hb, hna, k_mem, v_mem, g_mix, w2, g_mlstm, gmq, wpm, wpn, wpx, wout, tm, s):
    n, d = x2.shape
    m = k_mem.shape[1]
    per_b = s // tm
    row = lambda width: pl.BlockSpec((tm, width), lambda i: (i, 0))
    memblk = pl.BlockSpec((None, m, MEM_W), lambda i: (i // per_b, 0, 0))
    return pl.pallas_call(
        _merge_kernel,
        grid=(n // tm,),
        in_specs=[row(d), row(V_W), row(V_W), row(NA_W), memblk, memblk,
                  _const_spec((1, d)), _const_spec(w2.shape), _const_spec((1, V_W)),
                  _const_spec((1, MEM_W)), _const_spec((MEM_W, MEM_W)),
                  _const_spec(wpm.shape), _const_spec(wpn.shape), _const_spec(wpx.shape),
                  _const_spec(wout.shape)],
        out_specs=row(d),
        out_shape=jax.ShapeDtypeStruct((n, d), F32),
        compiler_params=pltpu.CompilerParams(dimension_semantics=("parallel",),
                                             vmem_limit_bytes=VMEM_LIMIT),
        name="merge",
    )(x2, hf, hb, hna, k_mem, v_mem, g_mix, w2, g_mlstm, gmq,
      _group_mean_matrix(MEM_W, MEM_HEAD_DIM), wpm, wpn, wpx, wout)


def _ffn_kernel(x_ref, g_ref, wu_ref, wd_ref, o_ref, *, n_chunks):
    x = x_ref[...]
    h = _rms_rows(x, g_ref[...]).astype(BF16)
    ck = wu_ref.shape[1] // n_chunks
    acc = x
    for j in range(n_chunks):
        u = jnp.maximum(_dot(h, wu_ref[:, j * ck:(j + 1) * ck]), 0.0)
        acc = acc + _dot((u * u).astype(BF16), wd_ref[j * ck:(j + 1) * ck, :])
    o_ref[...] = acc


def _ffn(x2, g_ffn, w_up, w_down, tm):
    n, d = x2.shape
    row = pl.BlockSpec((tm, d), lambda i: (i, 0))
    return pl.pallas_call(
        functools.partial(_ffn_kernel, n_chunks=w_up.shape[1] // d),
        grid=(n // tm,),
        in_specs=[row, _const_spec((1, d)), _const_spec(w_up.shape), _const_spec(w_down.shape)],
        out_specs=row,
        out_shape=jax.ShapeDtypeStruct((n, d), F32),
        compiler_params=pltpu.CompilerParams(dimension_semantics=("parallel",),
                                             vmem_limit_bytes=VMEM_LIMIT),
        name="ffn",
    )(x2, g_ffn, w_up, w_down)


def _layer(x, mem, g_mix, w_in, conv_w, conv_b, b_igate, b_fgate, g_mlstm, w_proj_mlstm,
           g_na_q, g_na_k, rpb, w_proj_na, g_mem, w_mem_kv, g_mem_q, g_mem_k,
           w_proj_mem, w_out, g_ffn, w_up, w_down):
    bsz, s, d = x.shape
    n = bsz * s
    tm = min(512, s)
    x2 = x.reshape(n, d)

    o_qk, o_v = 0, 2 * QK_W
    o_o = o_v + V_W
    o_i = o_o + V_W
    o_na = o_i + N_GATES
    o_qm = o_na + 3 * NA_W
    o_g = o_qm + MEM_W
    w_if = jnp.pad(w_in[:, o_i:o_na], ((0, 0), (0, GATE_PAD - N_GATES)))
    w1 = jnp.concatenate([w_in[:, o_qk:o_o], w_if, w_in[:, o_na:o_qm]], axis=1).astype(BF16)
    w2 = jnp.concatenate([w_in[:, o_o:o_i], w_in[:, o_qm:o_g], w_in[:, o_g:]], axis=1).astype(BF16)
    row = lambda a: a.reshape(1, -1).astype(F32)
    gate_b = jnp.pad(jnp.concatenate([b_igate.reshape(-1), b_fgate.reshape(-1)]).astype(F32),
                     (0, GATE_PAD - N_GATES)).reshape(1, GATE_PAD)

    qk, v, ifg, qn, kn, vn = _proj_in(x2, row(g_mix), w1, row(jnp.tile(g_na_q, NA_HEADS)),
                                      row(jnp.tile(g_na_k, NA_HEADS)), tm)
    k_mem, v_mem = _mem_kv(mem, row(g_mem), w_mem_kv.astype(BF16), row(jnp.tile(g_mem_k, MEM_HEADS)))

    L = min(MLSTM_CHUNK, s)
    hf, hb = _mlstm(qk.reshape(bsz, s, -1), v.reshape(bsz, s, -1), ifg.reshape(bsz, s, -1),
                    conv_w.astype(F32), row(conv_b), gate_b, L)
    hna = _natten(qn.reshape(bsz, s, -1), kn.reshape(bsz, s, -1), vn.reshape(bsz, s, -1),
                  _na_bias_table(rpb))

    x1 = _merge(x2, hf.reshape(n, -1), hb.reshape(n, -1), hna.reshape(n, -1), k_mem, v_mem,
                row(g_mix), w2, row(g_mlstm), row(jnp.tile(g_mem_q, MEM_HEADS)),
                w_proj_mlstm.astype(BF16), w_proj_na.astype(BF16), w_proj_mem.astype(BF16),
                w_out.astype(BF16), min(256, s), s)
    out = _ffn(x1, row(g_ffn), w_up.astype(BF16), w_down.astype(BF16), tm)
    return out.reshape(bsz, s, d)


def kernel(x, mem, g_mix, w_in, conv_w, conv_b, b_igate, b_fgate, g_mlstm, w_proj_mlstm,
           g_na_q, g_na_k, rpb, w_proj_na, g_mem, w_mem_kv, g_mem_q, g_mem_k,
           w_proj_mem, w_out, g_ffn, w_up, w_down):
    for l in range(g_mix.shape[0]):
        x = _layer(x, mem, g_mix[l], w_in[l], conv_w[l], conv_b[l], b_igate[l], b_fgate[l],
                   g_mlstm[l], w_proj_mlstm[l], g_na_q[l], g_na_k[l], rpb[l], w_proj_na[l],
                   g_mem[l], w_mem_kv[l], g_mem_q[l], g_mem_k[l], w_proj_mem[l], w_out[l],
                   g_ffn[l], w_up[l], w_down[l])
    return x
```

```python
import functools

import numpy as np
import jax
import jax.numpy as jnp
from jax import lax
from jax.experimental import pallas as pl
from jax.experimental.pallas import tpu as pltpu

GRID_W = 64
MLSTM_HEADS = 4
MLSTM_QK_DIM = 64
MLSTM_V_DIM = 128
MLSTM_CONV = 5
NA_HEADS = 8
NA_HEAD_DIM = 32
NA_WIN_ROWS = 8
NA_WIN_COLS = 16
MEM_HEADS = 4
MEM_HEAD_DIM = 64
N_BRANCH = 3
EPS = 1e-6

QK_W = MLSTM_HEADS * MLSTM_QK_DIM
V_W = MLSTM_HEADS * MLSTM_V_DIM
NA_W = NA_HEADS * NA_HEAD_DIM
MEM_W = MEM_HEADS * MEM_HEAD_DIM
N_GATES = 4 * MLSTM_HEADS
LANES = 128
SUBLANES = 8
HALO = SUBLANES
GATE_PAD = LANES
NEG_BIG = -1e30
LOG2E = 1.4426950408889634

MLSTM_CHUNK = 128
VMEM_LIMIT = 48 * 1024 * 1024

BF16 = jnp.bfloat16
F32 = jnp.float32


def _dot(a, b):
    return jnp.dot(a, b, preferred_element_type=F32)


def _dot_nt(a, b):
    return lax.dot_general(a, b, (((1,), (1,)), ((), ())), preferred_element_type=F32)


def _dot_tn(a, b):
    return lax.dot_general(a, b, (((0,), (0,)), ((), ())), preferred_element_type=F32)


def _split_dot(a, m_bf16):
    hi = a.astype(BF16)
    lo = (a - hi.astype(F32)).astype(BF16)
    return _dot(hi, m_bf16) + _dot(lo, m_bf16)


def _rms_rows(x, g):
    ms = jnp.mean(x * x, axis=-1, keepdims=True)
    return x * lax.rsqrt(ms + EPS) * g


def _group_mean_matrix(width, group):
    idx = np.arange(width) // group
    return jnp.asarray((idx[:, None] == idx[None, :]).astype(np.float32) / group, dtype=BF16)


def _const_spec(shape):
    nd = len(shape)
    return pl.BlockSpec(shape, lambda *_: (0,) * nd, pipeline_mode=pl.Buffered(1))


def _proj_in_kernel(x_ref, xp_ref, xn_ref, g_ref, w_ref, gm_ref, gq_ref, gk_ref, cw_ref, cb_ref,
                    qk_ref, v_ref, gi_ref, gf_ref, qn_ref, kn_ref, vn_ref, *, tiles_per_seq):
    i = pl.program_id(0)
    tm = x_ref.shape[0]
    g = g_ref[...]
    h = _rms_rows(x_ref[...], g).astype(BF16)
    pos = i % tiles_per_seq
    hp = jnp.where(pos > 0, _rms_rows(xp_ref[...], g), 0.0).astype(BF16)
    hn = jnp.where(pos < tiles_per_seq - 1, _rms_rows(xn_ref[...], g), 0.0).astype(BF16)
    o = 0
    z = _dot(jnp.concatenate([hp, h, hn], axis=0), w_ref[:, o:o + 2 * QK_W]); o += 2 * QK_W
    cw = cw_ref[...]
    acc = jnp.zeros((tm, 2 * QK_W), F32) + cb_ref[...]
    for j in range(MLSTM_CONV):
        r0 = HALO - MLSTM_CONV // 2 + j
        acc = acc + z[r0:r0 + tm, :] * cw[j:j + 1, :]
    act = acc * jax.nn.sigmoid(acc)
    lane = lax.broadcasted_iota(jnp.int32, act.shape, 1)
    qk_ref[...] = jnp.where(lane < QK_W, act * (MLSTM_QK_DIM ** -0.5), act).astype(BF16)
    v_ref[...] = _dot(h, w_ref[:, o:o + V_W]).astype(BF16); o += V_W
    gi_ref[...] = _dot(h, w_ref[:, o:o + GATE_PAD]); o += GATE_PAD
    gf_ref[...] = _dot(h, w_ref[:, o:o + GATE_PAD]); o += GATE_PAD
    q = _dot(h, w_ref[:, o:o + NA_W]); o += NA_W
    k = _dot(h, w_ref[:, o:o + NA_W]); o += NA_W
    vn_ref[...] = _dot(h, w_ref[:, o:o + NA_W]).astype(BF16)
    gm = gm_ref[...]
    qn = q * lax.rsqrt(_split_dot(q * q, gm) + EPS) * gq_ref[...]
    qn_ref[...] = (qn * (NA_HEAD_DIM ** -0.5)).astype(BF16)
    kn_ref[...] = (k * lax.rsqrt(_split_dot(k * k, gm) + EPS) * gk_ref[...]).astype(BF16)


def _proj_in(x2, g_mix, w1, gq, gk, conv_w, conv_b, tm, s):
    n, d = x2.shape
    w_cols = w1.shape[1]
    hb = tm // HALO
    row = lambda width: pl.BlockSpec((tm, width), lambda i: (i, 0))
    prev = pl.BlockSpec((HALO, d), lambda i: (jnp.maximum(i * hb - 1, 0), 0))
    nxt = pl.BlockSpec((HALO, d), lambda i: (jnp.minimum((i + 1) * hb, n // HALO - 1), 0))
    return pl.pallas_call(
        functools.partial(_proj_in_kernel, tiles_per_seq=s // tm),
        grid=(n // tm,),
        in_specs=[row(d), prev, nxt, _const_spec((1, d)), _const_spec((d, w_cols)),
                  _const_spec((NA_W, NA_W)), _const_spec((1, NA_W)), _const_spec((1, NA_W)),
                  _const_spec((MLSTM_CONV, 2 * QK_W)), _const_spec((1, 2 * QK_W))],
        out_specs=[row(2 * QK_W), row(V_W), row(GATE_PAD), row(GATE_PAD), row(NA_W), row(NA_W), row(NA_W)],
        out_shape=[jax.ShapeDtypeStruct((n, 2 * QK_W), BF16),
                   jax.ShapeDtypeStruct((n, V_W), BF16),
                   jax.ShapeDtypeStruct((n, GATE_PAD), F32),
                   jax.ShapeDtypeStruct((n, GATE_PAD), F32),
                   jax.ShapeDtypeStruct((n, NA_W), BF16),
                   jax.ShapeDtypeStruct((n, NA_W), BF16),
                   jax.ShapeDtypeStruct((n, NA_W), BF16)],
        compiler_params=pltpu.CompilerParams(dimension_semantics=("parallel",),
                                             vmem_limit_bytes=VMEM_LIMIT),
        name="proj_in",
    )(x2, x2, x2, g_mix, w1, _group_mean_matrix(NA_W, NA_HEAD_DIM), gq, gk, conv_w, conv_b)


def _mem_kv_kernel(mem_ref, g_ref, w_ref, gm_ref, gk_ref, k_ref, v_ref):
    h = _rms_rows(mem_ref[...], g_ref[...]).astype(BF16)
    k = _dot(h, w_ref[:, :MEM_W])
    v_ref[...] = _dot(h, w_ref[:, MEM_W:]).astype(BF16)
    k_ref[...] = (k * lax.rsqrt(_split_dot(k * k, gm_ref[...]) + EPS) * gk_ref[...]).astype(BF16)


def _mem_kv(mem, g_mem, w_kv, gk):
    b, m, d = mem.shape
    blk = lambda width: pl.BlockSpec((None, m, width), lambda i: (i, 0, 0))
    return pl.pallas_call(
        _mem_kv_kernel,
        grid=(b,),
        in_specs=[blk(d), _const_spec((1, d)), _const_spec((d, 2 * MEM_W)),
                  _const_spec((MEM_W, MEM_W)), _const_spec((1, MEM_W))],
        out_specs=[blk(MEM_W), blk(MEM_W)],
        out_shape=[jax.ShapeDtypeStruct((b, m, MEM_W), BF16)] * 2,
        compiler_params=pltpu.CompilerParams(dimension_semantics=("parallel",),
                                             vmem_limit_bytes=VMEM_LIMIT),
        name="mem_kv",
    )(mem, g_mem, w_kv, _group_mean_matrix(MEM_W, MEM_HEAD_DIM), gk)


def _log_sigmoid(x):
    return jnp.minimum(x, 0.0) - jnp.log1p(jnp.exp(-jnp.abs(x)))


def _cummax_rows(x, reverse):
    n = x.shape[0]
    row = lax.broadcasted_iota(jnp.int32, x.shape, 0)
    sh = 1
    while sh < n:
        if reverse:
            shifted, ok = pltpu.roll(x, n - sh, 0), row < n - sh
        else:
            shifted, ok = pltpu.roll(x, sh, 0), row >= sh
        x = jnp.where(ok, jnp.maximum(x, shifted), x)
        sh *= 2
    return x


def _mlstm_gates(reverse, ig, fpre, m_st):
    L = ig.shape[0]
    last = 0 if reverse else L - 1
    r_i = lax.broadcasted_iota(jnp.int32, (L, L), 0)
    c_i = lax.broadcasted_iota(jnp.int32, (L, L), 1)
    causal = (c_i >= r_i) if reverse else (c_i <= r_i)
    lf = _log_sigmoid(fpre)
    l1 = lf.astype(BF16)
    e1 = lf - l1.astype(F32)
    l2 = e1.astype(BF16)
    l3 = (e1 - l2.astype(F32)).astype(BF16)
    bb = _dot(jnp.where(causal, 1.0, 0.0).astype(BF16), jnp.concatenate([l1, l2, l3], axis=1))
    b = bb[:, :LANES] + bb[:, LANES:2 * LANES] + bb[:, 2 * LANES:]
    a = ig - b
    big_m = jnp.maximum(m_st, _cummax_rows(a, reverse))
    a2 = a * LOG2E
    m2 = big_m * LOG2E
    return dict(
        causal=causal, last=last,
        wint=jnp.exp(m_st - big_m),
        eneg=jnp.exp(-(b + big_m)),
        m2=m2,
        a2_rows=a2.T,
        ws=jnp.exp2(a2 - m2[last:last + 1, :]),
        m_next=(b + big_m)[last:last + 1, :],
    )


def _mlstm_chains(d, g, qk, v, states):
    L = qk.shape[0]
    H = MLSTM_HEADS
    causal, last = g["causal"], g["last"]
    lane = lax.broadcasted_iota(jnp.int32, (L, LANES), 1)
    nums, new_states = [], []
    den_all = jnp.zeros((L, LANES), F32)
    for hd in range(H):
        c = d * H + hd
        pair = hd // 2
        in_head = (lane // MLSTM_QK_DIM) == (hd % 2)
        q_h = jnp.where(in_head, qk[:, pair * LANES:(pair + 1) * LANES], jnp.zeros((), BF16))
        k_p = qk[:, QK_W + pair * LANES:QK_W + (pair + 1) * LANES]
        v_ext = jnp.concatenate([v[:, hd * LANES:(hd + 1) * LANES],
                                 jnp.where(lane == c, 1.0, 0.0).astype(BF16)], axis=1)
        st = states[hd]
        decay_log = g["a2_rows"][c:c + 1, :] - g["m2"][:, c:c + 1]
        p = jnp.exp2(jnp.where(causal, decay_log, -jnp.inf))
        s_mat = _dot_nt(q_h, k_p) * p
        q_inter = q_h.astype(F32) * g["wint"][:, c:c + 1]
        lhs = jnp.concatenate([s_mat.astype(BF16), q_inter.astype(BF16)], axis=1)
        rhs = jnp.concatenate([v_ext, st.astype(BF16)], axis=0)
        num_ext = _dot(lhs, rhs)
        nums.append(num_ext[:, :LANES])
        den_all = den_all + num_ext[:, LANES:]
        kw = (k_p.astype(F32) * g["ws"][:, c:c + 1]).astype(BF16)
        new_states.append(g["wint"][last:last + 1, c:c + 1] * st + _dot_tn(kw, v_ext))
    r_all = 1.0 / jnp.maximum(jnp.abs(den_all), g["eneg"])
    h = jnp.concatenate([nums[hd] * r_all[:, d * H + hd:d * H + hd + 1] for hd in range(H)], axis=1)
    return h, new_states


def _mlstm_kernel(qkf_ref, qkb_ref, vf_ref, vb_ref, gif_ref, gib_ref, gff_ref, gfb_ref, bi_ref, bf_ref,
                  hf_ref, hb_ref, st_ref, m_ref):
    H = MLSTM_HEADS

    @pl.when(pl.program_id(1) == 0)
    def _():
        st_ref[...] = jnp.zeros_like(st_ref)
        m_ref[...] = jnp.full(m_ref.shape, -jnp.inf, F32)

    bi, bf = bi_ref[...], bf_ref[...]
    g_f = _mlstm_gates(False, gif_ref[...] + bi, gff_ref[...] + bf, m_ref[0, 0:1, :])
    g_b = _mlstm_gates(True, gib_ref[...] + bi, gfb_ref[...] + bf, m_ref[1, 0:1, :])
    h_f, st_f = _mlstm_chains(0, g_f, qkf_ref[...], vf_ref[...], [st_ref[c] for c in range(H)])
    h_b, st_b = _mlstm_chains(1, g_b, qkb_ref[...], vb_ref[...], [st_ref[H + c] for c in range(H)])
    hf_ref[...] = h_f
    hb_ref[...] = h_b
    for c, st in enumerate(st_f + st_b):
        st_ref[c] = st
    m_ref[0] = jnp.broadcast_to(g_f["m_next"], m_ref.shape[1:])
    m_ref[1] = jnp.broadcast_to(g_b["m_next"], m_ref.shape[1:])


def _mlstm(qk3, v3, gi3, gf3, bias_i, bias_f, L):
    b, s, _ = qk3.shape
    nc = s // L
    fwd = lambda bi, c: (bi, c, 0)
    bwd = lambda bi, c: (bi, nc - 1 - c, 0)
    blk = lambda width, pos: pl.BlockSpec((None, L, width), pos)
    return pl.pallas_call(
        _mlstm_kernel,
        grid=(b, nc),
        in_specs=[blk(2 * QK_W, fwd), blk(2 * QK_W, bwd), blk(V_W, fwd), blk(V_W, bwd),
                  blk(GATE_PAD, fwd), blk(GATE_PAD, bwd), blk(GATE_PAD, fwd), blk(GATE_PAD, bwd),
                  _const_spec((1, GATE_PAD)), _const_spec((1, GATE_PAD))],
        out_specs=[blk(V_W, fwd), blk(V_W, bwd)],
        out_shape=[jax.ShapeDtypeStruct((b, s, V_W), F32)] * 2,
        scratch_shapes=[pltpu.VMEM((2 * MLSTM_HEADS, LANES, 2 * LANES), F32),
                        pltpu.VMEM((2, SUBLANES, LANES), F32)],
        compiler_params=pltpu.CompilerParams(dimension_semantics=("parallel", "arbitrary"),
                                             vmem_limit_bytes=VMEM_LIMIT),
        name="mlstm",
    )(qk3, qk3, v3, v3, gi3, gi3, gf3, gf3, bias_i, bias_f)


def _stack_heads(q, n_heads, head_dim):
    lane = lax.broadcasted_iota(jnp.int32, q.shape, 1)
    zero = jnp.zeros_like(q)
    return jnp.concatenate([jnp.where(lane // head_dim == h, q, zero) for h in range(n_heads)], axis=0)


def _unstack_heads(o, n_heads, head_dim):
    t = o.shape[0] // n_heads
    lane = lax.broadcasted_iota(jnp.int32, (t, o.shape[1]), 1)
    acc = jnp.zeros((t, o.shape[1]), o.dtype)
    for h in range(n_heads):
        acc = jnp.where(lane // head_dim == h, o[h * t:(h + 1) * t, :], acc)
    return acc


def _masked_attention(q_stacked, k, v, bias):
    sc = _dot_nt(q_stacked, k)
    if bias is not None:
        sc = sc + bias
    p = jnp.exp(sc - jnp.max(sc, axis=-1, keepdims=True))
    o = _dot(p.astype(BF16), v)
    return o / jnp.sum(p, axis=-1, keepdims=True)


def _natten_kernel(q_ref, k_ref, v_ref, bias_ref, o_ref, *, rows):
    r = pl.program_id(1)
    rs = jnp.clip(r - NA_WIN_ROWS // 2, 0, rows - NA_WIN_ROWS)
    start = pl.multiple_of(rs * GRID_W, GRID_W)
    k_win = k_ref[pl.ds(start, NA_WIN_ROWS * GRID_W), :]
    v_win = v_ref[pl.ds(start, NA_WIN_ROWS * GRID_W), :]
    o = _masked_attention(_stack_heads(q_ref[...], NA_HEADS, NA_HEAD_DIM), k_win, v_win, bias_ref[...])
    o_ref[...] = _unstack_heads(o, NA_HEADS, NA_HEAD_DIM).astype(o_ref.dtype)


def _na_bias_kernel(rpb_ref, onehot_ref, mask_ref, o_ref):
    r = rpb_ref[...]
    r1 = r.astype(BF16)
    e1 = r - r1.astype(F32)
    r2 = e1.astype(BF16)
    r3 = (e1 - r2.astype(F32)).astype(BF16)
    oh = onehot_ref[...]
    o_ref[...] = _dot(r1, oh) + _dot(r2, oh) + _dot(r3, oh) + mask_ref[...]


def _na_bias_table(rpb):
    n_ro, n_co = 2 * NA_WIN_ROWS - 1, 2 * NA_WIN_COLS - 1
    c = np.arange(GRID_W)[:, None]
    kc = np.arange(GRID_W)[None, :]
    cs = np.clip(c - NA_WIN_COLS // 2, 0, GRID_W - NA_WIN_COLS)
    valid = (kc >= cs) & (kc < cs + NA_WIN_COLS)
    col_off = kc - c + NA_WIN_COLS - 1
    onehot = (np.arange(LANES)[:, None, None] == col_off[None]) & valid[None]
    onehot = jnp.asarray(onehot.reshape(LANES, GRID_W * GRID_W), dtype=BF16)
    mask = jnp.asarray(np.where(valid, 0.0, NEG_BIG).reshape(1, GRID_W * GRID_W), dtype=F32)
    rp = jnp.pad(rpb.astype(F32).reshape(NA_HEADS * n_ro, n_co), ((0, 0), (0, LANES - n_co)))
    toep = pl.pallas_call(
        _na_bias_kernel,
        out_shape=jax.ShapeDtypeStruct((NA_HEADS * n_ro, GRID_W * GRID_W), F32),
        name="na_bias",
    )(rp, onehot, mask)
    toep = toep.reshape(NA_HEADS, n_ro, GRID_W, GRID_W).transpose(0, 2, 1, 3)
    toep = toep.reshape(NA_HEADS * GRID_W, n_ro * GRID_W)
    wk = NA_WIN_ROWS * GRID_W
    return jnp.stack([toep[:, (NA_WIN_ROWS - 1 - var) * GRID_W:(NA_WIN_ROWS - 1 - var) * GRID_W + wk]
                      for var in range(NA_WIN_ROWS)])


def _natten(qn, kn, vn, bias_tab):
    b, s, _ = qn.shape
    rows = s // GRID_W
    hc = NA_HEADS * GRID_W
    wk = NA_WIN_ROWS * GRID_W
    full = pl.BlockSpec((None, s, NA_W), lambda bi, r: (bi, 0, 0))

    def variant(bi, r):
        return (r - jnp.clip(r - NA_WIN_ROWS // 2, 0, rows - NA_WIN_ROWS), 0, 0)

    return pl.pallas_call(
        functools.partial(_natten_kernel, rows=rows),
        grid=(b, rows),
        in_specs=[pl.BlockSpec((None, GRID_W, NA_W), lambda bi, r: (bi, r, 0)), full, full,
                  pl.BlockSpec((None, hc, wk), variant)],
        out_specs=pl.BlockSpec((None, GRID_W, NA_W), lambda bi, r: (bi, r, 0)),
        out_shape=jax.ShapeDtypeStruct((b, s, NA_W), BF16),
        compiler_params=pltpu.CompilerParams(dimension_semantics=("parallel", "arbitrary"),
                                             vmem_limit_bytes=VMEM_LIMIT),
        name="natten",
    )(qn, kn, vn, bias_tab)


def _merge_kernel(x_ref, hf_ref, hb_ref, hna_ref, km_ref, vm_ref,
                  g_ref, w2_ref, gml_ref, gmq_ref, gm64_ref,
                  wpm_ref, wpn_ref, wpx_ref, wout_ref, o_ref):
    x = x_ref[...]
    d = x.shape[1]
    h = _rms_rows(x, g_ref[...]).astype(BF16)
    o_pre = _dot(h, w2_ref[:, :V_W])
    q_mem = _dot(h, w2_ref[:, V_W:V_W + MEM_W])
    g0 = V_W + MEM_W

    hm = hf_ref[...] + hb_ref[...]
    gml = gml_ref[...]
    parts = []
    for hd in range(MLSTM_HEADS):
        sl = slice(hd * MLSTM_V_DIM, (hd + 1) * MLSTM_V_DIM)
        parts.append(_rms_rows(hm[:, sl], gml[:, sl]))
    hm = (jnp.concatenate(parts, axis=1) * jax.nn.sigmoid(o_pre)).astype(BF16)
    y = jax.nn.sigmoid(_dot(h, w2_ref[:, g0:g0 + d])) * _dot(hm, wpm_ref[...])

    y = y + jax.nn.sigmoid(_dot(h, w2_ref[:, g0 + d:g0 + 2 * d])) * _dot(hna_ref[...], wpn_ref[...])

    qn = q_mem * lax.rsqrt(_split_dot(q_mem * q_mem, gm64_ref[...]) + EPS) * gmq_ref[...]
    qn = (qn * (MEM_HEAD_DIM ** -0.5)).astype(BF16)
    att = _masked_attention(_stack_heads(qn, MEM_HEADS, MEM_HEAD_DIM), km_ref[...], vm_ref[...], None)
    h_mem = _unstack_heads(att, MEM_HEADS, MEM_HEAD_DIM).astype(BF16)
    y = y + jax.nn.sigmoid(_dot(h, w2_ref[:, g0 + 2 * d:g0 + 3 * d])) * _dot(h_mem, wpx_ref[...])

    o_ref[...] = x + _dot(y.astype(BF16), wout_ref[...])


def _merge(x2, hf, hb, hna, k_mem, v_mem, g_mix, w2, g_mlstm, gmq, wpm, wpn, wpx, wout, tm, s):
    n, d = x2.shape
    m = k_mem.shape[1]
    per_b = s // tm
    row = lambda width: pl.BlockSpec((tm, width), lambda i: (i, 0))
    memblk = pl.BlockSpec((None, m, MEM_W), lambda i: (i // per_b, 0, 0))
    return pl.pallas_call(
        _merge_kernel,
        grid=(n // tm,),
        in_specs=[row(d), row(V_W), row(V_W), row(NA_W), memblk, memblk,
                  _const_spec((1, d)), _const_spec(w2.shape), _const_spec((1, V_W)),
                  _const_spec((1, MEM_W)), _const_spec((MEM_W, MEM_W)),
                  _const_spec(wpm.shape), _const_spec(wpn.shape), _const_spec(wpx.shape),
                  _const_spec(wout.shape)],
        out_specs=row(d),
        out_shape=jax.ShapeDtypeStruct((n, d), F32),
        compiler_params=pltpu.CompilerParams(dimension_semantics=("parallel",),
                                             vmem_limit_bytes=VMEM_LIMIT),
        name="merge",
    )(x2, hf, hb, hna, k_mem, v_mem, g_mix, w2, g_mlstm, gmq,
      _group_mean_matrix(MEM_W, MEM_HEAD_DIM), wpm, wpn, wpx, wout)


def _ffn_kernel(x_ref, g_ref, wu_ref, wd_ref, o_ref, *, n_chunks):
    x = x_ref[...]
    h = _rms_rows(x, g_ref[...]).astype(BF16)
    ck = wu_ref.shape[1] // n_chunks
    acc = x
    for j in range(n_chunks):
        u = jnp.maximum(_dot(h, wu_ref[:, j * ck:(j + 1) * ck]), 0.0)
        acc = acc + _dot((u * u).astype(BF16), wd_ref[j * ck:(j + 1) * ck, :])
    o_ref[...] = acc


def _ffn(x2, g_ffn, w_up, w_down, tm):
    n, d = x2.shape
    row = pl.BlockSpec((tm, d), lambda i: (i, 0))
    return pl.pallas_call(
        functools.partial(_ffn_kernel, n_chunks=w_up.shape[1] // d),
        grid=(n // tm,),
        in_specs=[row, _const_spec((1, d)), _const_spec(w_up.shape), _const_spec(w_down.shape)],
        out_specs=row,
        out_shape=jax.ShapeDtypeStruct((n, d), F32),
        compiler_params=pltpu.CompilerParams(dimension_semantics=("parallel",),
                                             vmem_limit_bytes=VMEM_LIMIT),
        name="ffn",
    )(x2, g_ffn, w_up, w_down)


def _layer(x, mem, g_mix, w_in, conv_w, conv_b, b_igate, b_fgate, g_mlstm, w_proj_mlstm,
           g_na_q, g_na_k, rpb, w_proj_na, g_mem, w_mem_kv, g_mem_q, g_mem_k,
           w_proj_mem, w_out, g_ffn, w_up, w_down):
    bsz, s, d = x.shape
    n = bsz * s
    tm = min(512, s)
    x2 = x.reshape(n, d)

    o_qk, o_v = 0, 2 * QK_W
    o_o = o_v + V_W
    o_i = o_o + V_W
    o_f = o_i + 2 * MLSTM_HEADS
    o_na = o_f + 2 * MLSTM_HEADS
    o_qm = o_na + 3 * NA_W
    o_g = o_qm + MEM_W
    gate_pad = ((0, 0), (0, GATE_PAD - N_GATES // 2))
    w_gi = jnp.pad(w_in[:, o_i:o_f], gate_pad)
    w_gf = jnp.pad(w_in[:, o_f:o_na], gate_pad)
    bias_i = jnp.pad(b_igate.reshape(1, -1).astype(F32), gate_pad)
    bias_f = jnp.pad(b_fgate.reshape(1, -1).astype(F32), gate_pad)
    w1 = jnp.concatenate([w_in[:, o_qk:o_o], w_gi, w_gf, w_in[:, o_na:o_qm]], axis=1).astype(BF16)
    w2 = jnp.concatenate([w_in[:, o_o:o_i], w_in[:, o_qm:o_g], w_in[:, o_g:]], axis=1).astype(BF16)
    row = lambda a: a.reshape(1, -1).astype(F32)

    qk, v, gi, gf, qn, kn, vn = _proj_in(x2, row(g_mix), w1, row(jnp.tile(g_na_q, NA_HEADS)),
                                         row(jnp.tile(g_na_k, NA_HEADS)), conv_w.astype(F32),
                                         row(conv_b), tm, s)
    k_mem, v_mem = _mem_kv(mem, row(g_mem), w_mem_kv.astype(BF16), row(jnp.tile(g_mem_k, MEM_HEADS)))

    L = min(MLSTM_CHUNK, s)
    hf, hb = _mlstm(qk.reshape(bsz, s, -1), v.reshape(bsz, s, -1), gi.reshape(bsz, s, -1),
                    gf.reshape(bsz, s, -1), bias_i, bias_f, L)
    hna = _natten(qn.reshape(bsz, s, -1), kn.reshape(bsz, s, -1), vn.reshape(bsz, s, -1),
                  _na_bias_table(rpb))

    x1 = _merge(x2, hf.reshape(n, -1), hb.reshape(n, -1), hna.reshape(n, -1), k_mem, v_mem,
                row(g_mix), w2, row(g_mlstm), row(jnp.tile(g_mem_q, MEM_HEADS)),
                w_proj_mlstm.astype(BF16), w_proj_na.astype(BF16), w_proj_mem.astype(BF16),
                w_out.astype(BF16), min(256, s), s)
    out = _ffn(x1, row(g_ffn), w_up.astype(BF16), w_down.astype(BF16), tm)
    return out.reshape(bsz, s, d)


def kernel(x, mem, g_mix, w_in, conv_w, conv_b, b_igate, b_fgate, g_mlstm, w_proj_mlstm,
           g_na_q, g_na_k, rpb, w_proj_na, g_mem, w_mem_kv, g_mem_q, g_mem_k,
           w_proj_mem, w_out, g_ffn, w_up, w_down):
    for l in range(g_mix.shape[0]):
        x = _layer(x, mem, g_mix[l], w_in[l], conv_w[l], conv_b[l], b_igate[l], b_fgate[l],
                   g_mlstm[l], w_proj_mlstm[l], g_na_q[l], g_na_k[l], rpb[l], w_proj_na[l],
                   g_mem[l], w_mem_kv[l], g_mem_q[l], g_mem_k[l], w_proj_mem[l], w_out[l],
                   g_ffn[l], w_up[l], w_down[l])
    return x
```

```python
import functools

import numpy as np
import jax
import jax.numpy as jnp
from jax import lax
from jax.experimental import pallas as pl
from jax.experimental.pallas import tpu as pltpu

GRID_W = 64
MLSTM_HEADS = 4
MLSTM_QK_DIM = 64
MLSTM_V_DIM = 128
MLSTM_CONV = 5
NA_HEADS = 8
NA_HEAD_DIM = 32
NA_WIN_ROWS = 8
NA_WIN_COLS = 16
MEM_HEADS = 4
MEM_HEAD_DIM = 64
N_BRANCH = 3
EPS = 1e-6

QK_W = MLSTM_HEADS * MLSTM_QK_DIM
V_W = MLSTM_HEADS * MLSTM_V_DIM
NA_W = NA_HEADS * NA_HEAD_DIM
MEM_W = MEM_HEADS * MEM_HEAD_DIM
N_GATES = 4 * MLSTM_HEADS
LANES = 128
SUBLANES = 8
HALO = SUBLANES
GATE_PAD = LANES
NEG_BIG = -1e30
LOG2E = 1.4426950408889634

MLSTM_CHUNK = 128
VMEM_LIMIT = 48 * 1024 * 1024

BF16 = jnp.bfloat16
F32 = jnp.float32


def _dot(a, b):
    return jnp.dot(a, b, preferred_element_type=F32)


def _dot_nt(a, b):
    return lax.dot_general(a, b, (((1,), (1,)), ((), ())), preferred_element_type=F32)


def _dot_tn(a, b):
    return lax.dot_general(a, b, (((0,), (0,)), ((), ())), preferred_element_type=F32)


def _split_dot(a, m_bf16):
    hi = a.astype(BF16)
    lo = (a - hi.astype(F32)).astype(BF16)
    return _dot(hi, m_bf16) + _dot(lo, m_bf16)


def _rms_rows(x, g):
    ms = jnp.mean(x * x, axis=-1, keepdims=True)
    return x * lax.rsqrt(ms + EPS) * g


def _group_mean_matrix(width, group):
    idx = np.arange(width) // group
    return jnp.asarray((idx[:, None] == idx[None, :]).astype(np.float32) / group, dtype=BF16)


def _const_spec(shape):
    nd = len(shape)
    return pl.BlockSpec(shape, lambda *_: (0,) * nd, pipeline_mode=pl.Buffered(1))


def _proj_in_kernel(x_ref, xp_ref, xn_ref, g_ref, w_ref, gm_ref, gq_ref, gk_ref, cw_ref, cb_ref,
                    qk_ref, v_ref, gi_ref, gf_ref, qn_ref, kn_ref, vn_ref, *, tiles_per_seq):
    i = pl.program_id(0)
    tm = x_ref.shape[0]
    g = g_ref[...]
    h = _rms_rows(x_ref[...], g).astype(BF16)
    pos = i % tiles_per_seq
    hp = jnp.where(pos > 0, _rms_rows(xp_ref[...], g), 0.0).astype(BF16)
    hn = jnp.where(pos < tiles_per_seq - 1, _rms_rows(xn_ref[...], g), 0.0).astype(BF16)
    o = 0
    z = _dot(jnp.concatenate([hp, h, hn], axis=0), w_ref[:, o:o + 2 * QK_W]); o += 2 * QK_W
    cw = cw_ref[...]
    acc = jnp.zeros((tm, 2 * QK_W), F32) + cb_ref[...]
    for j in range(MLSTM_CONV):
        r0 = HALO - MLSTM_CONV // 2 + j
        acc = acc + z[r0:r0 + tm, :] * cw[j:j + 1, :]
    act = acc * jax.nn.sigmoid(acc)
    lane = lax.broadcasted_iota(jnp.int32, act.shape, 1)
    qk_ref[...] = jnp.where(lane < QK_W, act * (MLSTM_QK_DIM ** -0.5), act).astype(BF16)
    v_ref[...] = _dot(h, w_ref[:, o:o + V_W]).astype(BF16); o += V_W
    gi_ref[...] = _dot(h, w_ref[:, o:o + GATE_PAD]); o += GATE_PAD
    gf_ref[...] = _dot(h, w_ref[:, o:o + GATE_PAD]); o += GATE_PAD
    q = _dot(h, w_ref[:, o:o + NA_W]); o += NA_W
    k = _dot(h, w_ref[:, o:o + NA_W]); o += NA_W
    vn_ref[...] = _dot(h, w_ref[:, o:o + NA_W]).astype(BF16)
    gm = gm_ref[...]
    qn = q * lax.rsqrt(_split_dot(q * q, gm) + EPS) * gq_ref[...]
    qn_ref[...] = (qn * (NA_HEAD_DIM ** -0.5 * LOG2E)).astype(BF16)
    kn_ref[...] = (k * lax.rsqrt(_split_dot(k * k, gm) + EPS) * gk_ref[...]).astype(BF16)


def _proj_in(x2, g_mix, w1, gq, gk, conv_w, conv_b, tm, s):
    n, d = x2.shape
    w_cols = w1.shape[1]
    hb = tm // HALO
    row = lambda width: pl.BlockSpec((tm, width), lambda i: (i, 0))
    prev = pl.BlockSpec((HALO, d), lambda i: (jnp.maximum(i * hb - 1, 0), 0))
    nxt = pl.BlockSpec((HALO, d), lambda i: (jnp.minimum((i + 1) * hb, n // HALO - 1), 0))
    return pl.pallas_call(
        functools.partial(_proj_in_kernel, tiles_per_seq=s // tm),
        grid=(n // tm,),
        in_specs=[row(d), prev, nxt, _const_spec((1, d)), _const_spec((d, w_cols)),
                  _const_spec((NA_W, NA_W)), _const_spec((1, NA_W)), _const_spec((1, NA_W)),
                  _const_spec((MLSTM_CONV, 2 * QK_W)), _const_spec((1, 2 * QK_W))],
        out_specs=[row(2 * QK_W), row(V_W), row(GATE_PAD), row(GATE_PAD), row(NA_W), row(NA_W), row(NA_W)],
        out_shape=[jax.ShapeDtypeStruct((n, 2 * QK_W), BF16),
                   jax.ShapeDtypeStruct((n, V_W), BF16),
                   jax.ShapeDtypeStruct((n, GATE_PAD), F32),
                   jax.ShapeDtypeStruct((n, GATE_PAD), F32),
                   jax.ShapeDtypeStruct((n, NA_W), BF16),
                   jax.ShapeDtypeStruct((n, NA_W), BF16),
                   jax.ShapeDtypeStruct((n, NA_W), BF16)],
        compiler_params=pltpu.CompilerParams(dimension_semantics=("parallel",),
                                             vmem_limit_bytes=VMEM_LIMIT),
        name="proj_in",
    )(x2, x2, x2, g_mix, w1, _group_mean_matrix(NA_W, NA_HEAD_DIM), gq, gk, conv_w, conv_b)


def _mem_kv_kernel(mem_ref, g_ref, w_ref, gm_ref, gk_ref, k_ref, v_ref):
    h = _rms_rows(mem_ref[...], g_ref[...]).astype(BF16)
    k = _dot(h, w_ref[:, :MEM_W])
    v_ref[...] = _dot(h, w_ref[:, MEM_W:]).astype(BF16)
    k_ref[...] = (k * lax.rsqrt(_split_dot(k * k, gm_ref[...]) + EPS) * gk_ref[...]).astype(BF16)


def _mem_kv(mem, g_mem, w_kv, gk):
    b, m, d = mem.shape
    blk = lambda width: pl.BlockSpec((None, m, width), lambda i: (i, 0, 0))
    return pl.pallas_call(
        _mem_kv_kernel,
        grid=(b,),
        in_specs=[blk(d), _const_spec((1, d)), _const_spec((d, 2 * MEM_W)),
                  _const_spec((MEM_W, MEM_W)), _const_spec((1, MEM_W))],
        out_specs=[blk(MEM_W), blk(MEM_W)],
        out_shape=[jax.ShapeDtypeStruct((b, m, MEM_W), BF16)] * 2,
        compiler_params=pltpu.CompilerParams(dimension_semantics=("parallel",),
                                             vmem_limit_bytes=VMEM_LIMIT),
        name="mem_kv",
    )(mem, g_mem, w_kv, _group_mean_matrix(MEM_W, MEM_HEAD_DIM), gk)


def _log_sigmoid(x):
    return jnp.minimum(x, 0.0) - jnp.log1p(jnp.exp(-jnp.abs(x)))


def _cummax_rows(x, reverse):
    n = x.shape[0]
    row = lax.broadcasted_iota(jnp.int32, x.shape, 0)
    sh = 1
    while sh < n:
        if reverse:
            shifted, ok = pltpu.roll(x, n - sh, 0), row < n - sh
        else:
            shifted, ok = pltpu.roll(x, sh, 0), row >= sh
        x = jnp.where(ok, jnp.maximum(x, shifted), x)
        sh *= 2
    return x


def _mlstm_gates(reverse, ig, fpre, m_st):
    L = ig.shape[0]
    last = 0 if reverse else L - 1
    r_i = lax.broadcasted_iota(jnp.int32, (L, L), 0)
    c_i = lax.broadcasted_iota(jnp.int32, (L, L), 1)
    causal = (c_i >= r_i) if reverse else (c_i <= r_i)
    lf = _log_sigmoid(fpre)
    l1 = lf.astype(BF16)
    e1 = lf - l1.astype(F32)
    l2 = e1.astype(BF16)
    l3 = (e1 - l2.astype(F32)).astype(BF16)
    bb = _dot(jnp.where(causal, 1.0, 0.0).astype(BF16), jnp.concatenate([l1, l2, l3], axis=1))
    b = bb[:, :LANES] + bb[:, LANES:2 * LANES] + bb[:, 2 * LANES:]
    a = ig - b
    big_m = jnp.maximum(m_st, _cummax_rows(a, reverse))
    a2 = a * LOG2E
    m2 = big_m * LOG2E
    return dict(
        causal=causal, last=last,
        wint=jnp.exp(m_st - big_m),
        eneg=jnp.exp(-(b + big_m)),
        m2=m2,
        a2_rows=a2.T,
        ws=jnp.exp2(a2 - m2[last:last + 1, :]),
        m_next=(b + big_m)[last:last + 1, :],
    )


def _mlstm_chains(d, g, qk, v, states):
    L = qk.shape[0]
    H = MLSTM_HEADS
    causal, last = g["causal"], g["last"]
    lane = lax.broadcasted_iota(jnp.int32, (L, LANES), 1)
    nums, new_states = [], []
    den_all = jnp.zeros((L, LANES), F32)
    for hd in range(H):
        c = d * H + hd
        pair = hd // 2
        in_head = (lane // MLSTM_QK_DIM) == (hd % 2)
        q_h = jnp.where(in_head, qk[:, pair * LANES:(pair + 1) * LANES], jnp.zeros((), BF16))
        k_p = qk[:, QK_W + pair * LANES:QK_W + (pair + 1) * LANES]
        v_ext = jnp.concatenate([v[:, hd * LANES:(hd + 1) * LANES],
                                 jnp.where(lane == c, 1.0, 0.0).astype(BF16)], axis=1)
        st = states[hd]
        decay_log = g["a2_rows"][c:c + 1, :] - g["m2"][:, c:c + 1]
        p = jnp.exp2(jnp.where(causal, decay_log, -jnp.inf))
        s_mat = _dot_nt(q_h, k_p) * p
        q_inter = q_h.astype(F32) * g["wint"][:, c:c + 1]
        lhs = jnp.concatenate([s_mat.astype(BF16), q_inter.astype(BF16)], axis=1)
        rhs = jnp.concatenate([v_ext, st.astype(BF16)], axis=0)
        num_ext = _dot(lhs, rhs)
        nums.append(num_ext[:, :LANES])
        den_all = den_all + num_ext[:, LANES:]
        kw = (k_p.astype(F32) * g["ws"][:, c:c + 1]).astype(BF16)
        new_states.append(g["wint"][last:last + 1, c:c + 1] * st + _dot_tn(kw, v_ext))
    r_all = 1.0 / jnp.maximum(jnp.abs(den_all), g["eneg"])
    h = jnp.concatenate([nums[hd] * r_all[:, d * H + hd:d * H + hd + 1] for hd in range(H)], axis=1)
    return h, new_states


def _mlstm_kernel(qkf_ref, qkb_ref, vf_ref, vb_ref, gif_ref, gib_ref, gff_ref, gfb_ref, bi_ref, bf_ref,
                  hf_ref, hb_ref, st_ref, m_ref):
    H = MLSTM_HEADS

    @pl.when(pl.program_id(1) == 0)
    def _():
        st_ref[...] = jnp.zeros_like(st_ref)
        m_ref[...] = jnp.full(m_ref.shape, -jnp.inf, F32)

    bi, bf = bi_ref[...], bf_ref[...]
    results = []
    for e in range(qkf_ref.shape[0]):
        g_f = _mlstm_gates(False, gif_ref[e] + bi, gff_ref[e] + bf, m_ref[2 * e, 0:1, :])
        g_b = _mlstm_gates(True, gib_ref[e] + bi, gfb_ref[e] + bf, m_ref[2 * e + 1, 0:1, :])
        base = 2 * H * e
        h_f, st_f = _mlstm_chains(0, g_f, qkf_ref[e], vf_ref[e], [st_ref[base + c] for c in range(H)])
        h_b, st_b = _mlstm_chains(1, g_b, qkb_ref[e], vb_ref[e], [st_ref[base + H + c] for c in range(H)])
        results.append((h_f, h_b, st_f + st_b, g_f["m_next"], g_b["m_next"]))
    for e, (h_f, h_b, sts, m_f, m_b) in enumerate(results):
        hf_ref[e] = h_f
        hb_ref[e] = h_b
        for c, st in enumerate(sts):
            st_ref[2 * H * e + c] = st
        m_ref[2 * e] = jnp.broadcast_to(m_f, m_ref.shape[1:])
        m_ref[2 * e + 1] = jnp.broadcast_to(m_b, m_ref.shape[1:])


def _mlstm(qk3, v3, gi3, gf3, bias_i, bias_f, L, nb):
    b, s, _ = qk3.shape
    nc = s // L
    fwd = lambda bi, c: (bi, c, 0)
    bwd = lambda bi, c: (bi, nc - 1 - c, 0)
    blk = lambda width, pos: pl.BlockSpec((nb, L, width), pos)
    return pl.pallas_call(
        _mlstm_kernel,
        grid=(b // nb, nc),
        in_specs=[blk(2 * QK_W, fwd), blk(2 * QK_W, bwd), blk(V_W, fwd), blk(V_W, bwd),
                  blk(GATE_PAD, fwd), blk(GATE_PAD, bwd), blk(GATE_PAD, fwd), blk(GATE_PAD, bwd),
                  _const_spec((1, GATE_PAD)), _const_spec((1, GATE_PAD))],
        out_specs=[blk(V_W, fwd), blk(V_W, bwd)],
        out_shape=[jax.ShapeDtypeStruct((b, s, V_W), F32)] * 2,
        scratch_shapes=[pltpu.VMEM((nb * 2 * MLSTM_HEADS, LANES, 2 * LANES), F32),
                        pltpu.VMEM((nb * 2, SUBLANES, LANES), F32)],
        compiler_params=pltpu.CompilerParams(dimension_semantics=("parallel", "arbitrary"),
                                             vmem_limit_bytes=VMEM_LIMIT),
        name="mlstm",
    )(qk3, qk3, v3, v3, gi3, gi3, gf3, gf3, bias_i, bias_f)


def _stack_heads(q, n_heads, head_dim):
    lane = lax.broadcasted_iota(jnp.int32, q.shape, 1)
    zero = jnp.zeros_like(q)
    return jnp.concatenate([jnp.where(lane // head_dim == h, q, zero) for h in range(n_heads)], axis=0)


def _unstack_heads(o, n_heads, head_dim):
    t = o.shape[0] // n_heads
    lane = lax.broadcasted_iota(jnp.int32, (t, o.shape[1]), 1)
    acc = jnp.zeros((t, o.shape[1]), o.dtype)
    for h in range(n_heads):
        acc = jnp.where(lane // head_dim == h, o[h * t:(h + 1) * t, :], acc)
    return acc


def _masked_attention(q_stacked, k, v, bias):
    sc = _dot_nt(q_stacked, k)
    if bias is not None:
        sc = sc + bias
    p = jnp.exp2(sc - jnp.max(sc, axis=-1, keepdims=True))
    o = _dot(p.astype(BF16), v)
    return o / jnp.sum(p, axis=-1, keepdims=True)


def _natten_kernel(q_ref, k_ref, v_ref, bias_ref, o_ref, *, rows, rows_per_step):
    for j in range(rows_per_step):
        r = pl.program_id(1) * rows_per_step + j
        rs = jnp.clip(r - NA_WIN_ROWS // 2, 0, rows - NA_WIN_ROWS)
        start = pl.multiple_of(rs * GRID_W, GRID_W)
        k_win = k_ref[pl.ds(start, NA_WIN_ROWS * GRID_W), :]
        v_win = v_ref[pl.ds(start, NA_WIN_ROWS * GRID_W), :]
        q = q_ref[j * GRID_W:(j + 1) * GRID_W, :]
        o = _masked_attention(_stack_heads(q, NA_HEADS, NA_HEAD_DIM), k_win, v_win, bias_ref[r - rs])
        o_ref[j * GRID_W:(j + 1) * GRID_W, :] = _unstack_heads(o, NA_HEADS, NA_HEAD_DIM).astype(o_ref.dtype)


def _na_bias_kernel(rpb_ref, onehot_ref, mask_ref, o_ref):
    r = rpb_ref[...]
    r1 = r.astype(BF16)
    e1 = r - r1.astype(F32)
    r2 = e1.astype(BF16)
    r3 = (e1 - r2.astype(F32)).astype(BF16)
    oh = onehot_ref[...]
    o_ref[...] = (_dot(r1, oh) + _dot(r2, oh) + _dot(r3, oh)) * LOG2E + mask_ref[...]


def _na_bias_table(rpb):
    n_ro, n_co = 2 * NA_WIN_ROWS - 1, 2 * NA_WIN_COLS - 1
    c = np.arange(GRID_W)[:, None]
    kc = np.arange(GRID_W)[None, :]
    cs = np.clip(c - NA_WIN_COLS // 2, 0, GRID_W - NA_WIN_COLS)
    valid = (kc >= cs) & (kc < cs + NA_WIN_COLS)
    col_off = kc - c + NA_WIN_COLS - 1
    onehot = (np.arange(LANES)[:, None, None] == col_off[None]) & valid[None]
    onehot = jnp.asarray(onehot.reshape(LANES, GRID_W * GRID_W), dtype=BF16)
    mask = jnp.asarray(np.where(valid, 0.0, NEG_BIG).reshape(1, GRID_W * GRID_W), dtype=F32)
    rp = jnp.pad(rpb.astype(F32).reshape(NA_HEADS * n_ro, n_co), ((0, 0), (0, LANES - n_co)))
    toep = pl.pallas_call(
        _na_bias_kernel,
        out_shape=jax.ShapeDtypeStruct((NA_HEADS * n_ro, GRID_W * GRID_W), F32),
        name="na_bias",
    )(rp, onehot, mask)
    toep = toep.reshape(NA_HEADS, n_ro, GRID_W, GRID_W).transpose(0, 2, 1, 3)
    toep = toep.reshape(NA_HEADS * GRID_W, n_ro * GRID_W)
    wk = NA_WIN_ROWS * GRID_W
    return jnp.stack([toep[:, (NA_WIN_ROWS - 1 - var) * GRID_W:(NA_WIN_ROWS - 1 - var) * GRID_W + wk]
                      for var in range(NA_WIN_ROWS)])


def _natten(qn, kn, vn, bias_tab, rows_per_step):
    b, s, _ = qn.shape
    rows = s // GRID_W
    full = pl.BlockSpec((None, s, NA_W), lambda bi, r: (bi, 0, 0))
    tile = pl.BlockSpec((None, rows_per_step * GRID_W, NA_W), lambda bi, r: (bi, r, 0))
    return pl.pallas_call(
        functools.partial(_natten_kernel, rows=rows, rows_per_step=rows_per_step),
        grid=(b, rows // rows_per_step),
        in_specs=[tile, full, full, _const_spec(bias_tab.shape)],
        out_specs=tile,
        out_shape=jax.ShapeDtypeStruct((b, s, NA_W), BF16),
        compiler_params=pltpu.CompilerParams(dimension_semantics=("parallel", "arbitrary"),
                                             vmem_limit_bytes=VMEM_LIMIT),
        name="natten",
    )(qn, kn, vn, bias_tab)


def _merge_kernel(x_ref, hf_ref, hb_ref, hna_ref, km_ref, vm_ref,
                  g_ref, w2_ref, gml_ref, gmq_ref, gm64_ref,
                  wpm_ref, wpn_ref, wpx_ref, wout_ref, o_ref):
    x = x_ref[...]
    d = x.shape[1]
    h = _rms_rows(x, g_ref[...]).astype(BF16)
    o_pre = _dot(h, w2_ref[:, :V_W])
    q_mem = _dot(h, w2_ref[:, V_W:V_W + MEM_W])
    g0 = V_W + MEM_W

    hm = hf_ref[...] + hb_ref[...]
    gml = gml_ref[...]
    parts = []
    for hd in range(MLSTM_HEADS):
        sl = slice(hd * MLSTM_V_DIM, (hd + 1) * MLSTM_V_DIM)
        parts.append(_rms_rows(hm[:, sl], gml[:, sl]))
    hm = (jnp.concatenate(parts, axis=1) * jax.nn.sigmoid(o_pre)).astype(BF16)
    y = jax.nn.sigmoid(_dot(h, w2_ref[:, g0:g0 + d])) * _dot(hm, wpm_ref[...])

    y = y + jax.nn.sigmoid(_dot(h, w2_ref[:, g0 + d:g0 + 2 * d])) * _dot(hna_ref[...], wpn_ref[...])

    qn = q_mem * lax.rsqrt(_split_dot(q_mem * q_mem, gm64_ref[...]) + EPS) * gmq_ref[...]
    qn = (qn * (MEM_HEAD_DIM ** -0.5 * LOG2E)).astype(BF16)
    att = _masked_attention(_stack_heads(qn, MEM_HEADS, MEM_HEAD_DIM), km_ref[...], vm_ref[...], None)
    h_mem = _unstack_heads(att, MEM_HEADS, MEM_HEAD_DIM).astype(BF16)
    y = y + jax.nn.sigmoid(_dot(h, w2_ref[:, g0 + 2 * d:g0 + 3 * d])) * _dot(h_mem, wpx_ref[...])

    o_ref[...] = x + _dot(y.astype(BF16), wout_ref[...])


def _merge(x2, hf, hb, hna, k_mem, v_mem, g_mix, w2, g_mlstm, gmq, wpm, wpn, wpx, wout, tm, s):
    n, d = x2.shape
    m = k_mem.shape[1]
    per_b = s // tm
    row = lambda width: pl.BlockSpec((tm, width), lambda i: (i, 0))
    memblk = pl.BlockSpec((None, m, MEM_W), lambda i: (i // per_b, 0, 0))
    return pl.pallas_call(
        _merge_kernel,
        grid=(n // tm,),
        in_specs=[row(d), row(V_W), row(V_W), row(NA_W), memblk, memblk,
                  _const_spec((1, d)), _const_spec(w2.shape), _const_spec((1, V_W)),
                  _const_spec((1, MEM_W)), _const_spec((MEM_W, MEM_W)),
                  _const_spec(wpm.shape), _const_spec(wpn.shape), _const_spec(wpx.shape),
                  _const_spec(wout.shape)],
        out_specs=row(d),
        out_shape=jax.ShapeDtypeStruct((n, d), F32),
        compiler_params=pltpu.CompilerParams(dimension_semantics=("parallel",),
                                             vmem_limit_bytes=VMEM_LIMIT),
        name="merge",
    )(x2, hf, hb, hna, k_mem, v_mem, g_mix, w2, g_mlstm, gmq,
      _group_mean_matrix(MEM_W, MEM_HEAD_DIM), wpm, wpn, wpx, wout)


def _ffn_kernel(x_ref, g_ref, wu_ref, wd_ref, o_ref, *, n_chunks):
    x = x_ref[...]
    h = _rms_rows(x, g_ref[...]).astype(BF16)
    ck = wu_ref.shape[1] // n_chunks
    acc = x
    for j in range(n_chunks):
        u = jnp.maximum(_dot(h, wu_ref[:, j * ck:(j + 1) * ck]), 0.0)
        acc = acc + _dot((u * u).astype(BF16), wd_ref[j * ck:(j + 1) * ck, :])
    o_ref[...] = acc


def _ffn(x2, g_ffn, w_up, w_down, tm):
    n, d = x2.shape
    row = pl.BlockSpec((tm, d), lambda i: (i, 0))
    return pl.pallas_call(
        functools.partial(_ffn_kernel, n_chunks=w_up.shape[1] // d),
        grid=(n // tm,),
        in_specs=[row, _const_spec((1, d)), _const_spec(w_up.shape), _const_spec(w_down.shape)],
        out_specs=row,
        out_shape=jax.ShapeDtypeStruct((n, d), F32),
        compiler_params=pltpu.CompilerParams(dimension_semantics=("parallel",),
                                             vmem_limit_bytes=VMEM_LIMIT),
        name="ffn",
    )(x2, g_ffn, w_up, w_down)


def _layer(x, mem, g_mix, w_in, conv_w, conv_b, b_igate, b_fgate, g_mlstm, w_proj_mlstm,
           g_na_q, g_na_k, rpb, w_proj_na, g_mem, w_mem_kv, g_mem_q, g_mem_k,
           w_proj_mem, w_out, g_ffn, w_up, w_down):
    bsz, s, d = x.shape
    n = bsz * s
    tm = min(512, s)
    x2 = x.reshape(n, d)

    o_qk, o_v = 0, 2 * QK_W
    o_o = o_v + V_W
    o_i = o_o + V_W
    o_f = o_i + 2 * MLSTM_HEADS
    o_na = o_f + 2 * MLSTM_HEADS
    o_qm = o_na + 3 * NA_W
    o_g = o_qm + MEM_W
    gate_pad = ((0, 0), (0, GATE_PAD - N_GATES // 2))
    w_gi = jnp.pad(w_in[:, o_i:o_f], gate_pad)
    w_gf = jnp.pad(w_in[:, o_f:o_na], gate_pad)
    bias_i = jnp.pad(b_igate.reshape(1, -1).astype(F32), gate_pad)
    bias_f = jnp.pad(b_fgate.reshape(1, -1).astype(F32), gate_pad)
    w1 = jnp.concatenate([w_in[:, o_qk:o_o], w_gi, w_gf, w_in[:, o_na:o_qm]], axis=1).astype(BF16)
    w2 = jnp.concatenate([w_in[:, o_o:o_i], w_in[:, o_qm:o_g], w_in[:, o_g:]], axis=1).astype(BF16)
    row = lambda a: a.reshape(1, -1).astype(F32)

    qk, v, gi, gf, qn, kn, vn = _proj_in(x2, row(g_mix), w1, row(jnp.tile(g_na_q, NA_HEADS)),
                                         row(jnp.tile(g_na_k, NA_HEADS)), conv_w.astype(F32),
                                         row(conv_b), tm, s)
    k_mem, v_mem = _mem_kv(mem, row(g_mem), w_mem_kv.astype(BF16), row(jnp.tile(g_mem_k, MEM_HEADS)))

    L = min(MLSTM_CHUNK, s)
    hf, hb = _mlstm(qk.reshape(bsz, s, -1), v.reshape(bsz, s, -1), gi.reshape(bsz, s, -1),
                    gf.reshape(bsz, s, -1), bias_i, bias_f, L, 2 if bsz % 2 == 0 else 1)
    hna = _natten(qn.reshape(bsz, s, -1), kn.reshape(bsz, s, -1), vn.reshape(bsz, s, -1),
                  _na_bias_table(rpb), 4 if (s // GRID_W) % 4 == 0 else 1)

    x1 = _merge(x2, hf.reshape(n, -1), hb.reshape(n, -1), hna.reshape(n, -1), k_mem, v_mem,
                row(g_mix), w2, row(g_mlstm), row(jnp.tile(g_mem_q, MEM_HEADS)),
                w_proj_mlstm.astype(BF16), w_proj_na.astype(BF16), w_proj_mem.astype(BF16),
                w_out.astype(BF16), min(256, s), s)
    out = _ffn(x1, row(g_ffn), w_up.astype(BF16), w_down.astype(BF16), tm)
    return out.reshape(bsz, s, d)


def kernel(x, mem, g_mix, w_in, conv_w, conv_b, b_igate, b_fgate, g_mlstm, w_proj_mlstm,
           g_na_q, g_na_k, rpb, w_proj_na, g_mem, w_mem_kv, g_mem_q, g_mem_k,
           w_proj_mem, w_out, g_ffn, w_up, w_down):
    for l in range(g_mix.shape[0]):
        x = _layer(x, mem, g_mix[l], w_in[l], conv_w[l], conv_b[l], b_igate[l], b_fgate[l],
                   g_mlstm[l], w_proj_mlstm[l], g_na_q[l], g_na_k[l], rpb[l], w_proj_na[l],
                   g_mem[l], w_mem_kv[l], g_mem_q[l], g_mem_k[l], w_proj_mem[l], w_out[l],
                   g_ffn[l], w_up[l], w_down[l])
    return x
```

```python
import functools

import numpy as np
import jax
import jax.numpy as jnp
from jax import lax
from jax.experimental import pallas as pl
from jax.experimental.pallas import tpu as pltpu

GRID_W = 64
MLSTM_HEADS = 4
MLSTM_QK_DIM = 64
MLSTM_V_DIM = 128
MLSTM_CONV = 5
NA_HEADS = 8
NA_HEAD_DIM = 32
NA_WIN_ROWS = 8
NA_WIN_COLS = 16
MEM_HEADS = 4
MEM_HEAD_DIM = 64
N_BRANCH = 3
EPS = 1e-6

QK_W = MLSTM_HEADS * MLSTM_QK_DIM
V_W = MLSTM_HEADS * MLSTM_V_DIM
NA_W = NA_HEADS * NA_HEAD_DIM
MEM_W = MEM_HEADS * MEM_HEAD_DIM
N_GATES = 4 * MLSTM_HEADS
LANES = 128
SUBLANES = 8
HALO = SUBLANES
GATE_PAD = LANES
NEG_BIG = -1e30
LOG2E = 1.4426950408889634

MLSTM_CHUNK = 256
VMEM_LIMIT = 48 * 1024 * 1024

BF16 = jnp.bfloat16
F32 = jnp.float32


def _dot(a, b):
    return jnp.dot(a, b, preferred_element_type=F32)


def _dot_nt(a, b):
    return lax.dot_general(a, b, (((1,), (1,)), ((), ())), preferred_element_type=F32)


def _dot_tn(a, b):
    return lax.dot_general(a, b, (((0,), (0,)), ((), ())), preferred_element_type=F32)


def _split_dot(a, m_bf16):
    hi = a.astype(BF16)
    lo = (a - hi.astype(F32)).astype(BF16)
    return _dot(hi, m_bf16) + _dot(lo, m_bf16)


def _rms_rows(x, g):
    ms = jnp.mean(x * x, axis=-1, keepdims=True)
    return x * lax.rsqrt(ms + EPS) * g


def _group_mean_matrix(width, group):
    idx = np.arange(width) // group
    return jnp.asarray((idx[:, None] == idx[None, :]).astype(np.float32) / group, dtype=BF16)


def _const_spec(shape):
    nd = len(shape)
    return pl.BlockSpec(shape, lambda *_: (0,) * nd, pipeline_mode=pl.Buffered(1))


def _proj_in_kernel(x_ref, xp_ref, xn_ref, g_ref, w_ref, gm_ref, gq_ref, gk_ref, cw_ref, cb_ref,
                    qk_ref, v_ref, gi_ref, gf_ref, qn_ref, kn_ref, vn_ref, *, tiles_per_seq):
    i = pl.program_id(0)
    tm = x_ref.shape[0]
    g = g_ref[...]
    h = _rms_rows(x_ref[...], g).astype(BF16)
    pos = i % tiles_per_seq
    hp = jnp.where(pos > 0, _rms_rows(xp_ref[...], g), 0.0).astype(BF16)
    hn = jnp.where(pos < tiles_per_seq - 1, _rms_rows(xn_ref[...], g), 0.0).astype(BF16)
    o = 0
    z = _dot(jnp.concatenate([hp, h, hn], axis=0), w_ref[:, o:o + 2 * QK_W]); o += 2 * QK_W
    cw = cw_ref[...]
    acc = jnp.zeros((tm, 2 * QK_W), F32) + cb_ref[...]
    for j in range(MLSTM_CONV):
        d = j - MLSTM_CONV // 2
        tap = z if d == 0 else pltpu.roll(z, (-d) % z.shape[0], 0)
        acc = acc + tap[HALO:HALO + tm, :] * cw[j:j + 1, :]
    act = acc * jax.nn.sigmoid(acc)
    lane = lax.broadcasted_iota(jnp.int32, act.shape, 1)
    qk_ref[...] = jnp.where(lane < QK_W, act * (MLSTM_QK_DIM ** -0.5), act).astype(BF16)
    v_ref[...] = _dot(h, w_ref[:, o:o + V_W]).astype(BF16); o += V_W
    gi_ref[...] = _dot(h, w_ref[:, o:o + GATE_PAD]); o += GATE_PAD
    gf_ref[...] = _dot(h, w_ref[:, o:o + GATE_PAD]); o += GATE_PAD
    q = _dot(h, w_ref[:, o:o + NA_W]); o += NA_W
    k = _dot(h, w_ref[:, o:o + NA_W]); o += NA_W
    vn_ref[...] = _dot(h, w_ref[:, o:o + NA_W]).astype(BF16)
    gm = gm_ref[...]
    qn = q * lax.rsqrt(_split_dot(q * q, gm) + EPS) * gq_ref[...]
    qn_ref[...] = (qn * (NA_HEAD_DIM ** -0.5 * LOG2E)).astype(BF16)
    kn_ref[...] = (k * lax.rsqrt(_split_dot(k * k, gm) + EPS) * gk_ref[...]).astype(BF16)


def _proj_in(x2, g_mix, w1, gq, gk, conv_w, conv_b, tm, s):
    n, d = x2.shape
    w_cols = w1.shape[1]
    hb = tm // HALO
    row = lambda width: pl.BlockSpec((tm, width), lambda i: (i, 0))
    prev = pl.BlockSpec((HALO, d), lambda i: (jnp.maximum(i * hb - 1, 0), 0))
    nxt = pl.BlockSpec((HALO, d), lambda i: (jnp.minimum((i + 1) * hb, n // HALO - 1), 0))
    return pl.pallas_call(
        functools.partial(_proj_in_kernel, tiles_per_seq=s // tm),
        grid=(n // tm,),
        in_specs=[row(d), prev, nxt, _const_spec((1, d)), _const_spec((d, w_cols)),
                  _const_spec((NA_W, NA_W)), _const_spec((1, NA_W)), _const_spec((1, NA_W)),
                  _const_spec((MLSTM_CONV, 2 * QK_W)), _const_spec((1, 2 * QK_W))],
        out_specs=[row(2 * QK_W), row(V_W), row(GATE_PAD), row(GATE_PAD), row(NA_W), row(NA_W), row(NA_W)],
        out_shape=[jax.ShapeDtypeStruct((n, 2 * QK_W), BF16),
                   jax.ShapeDtypeStruct((n, V_W), BF16),
                   jax.ShapeDtypeStruct((n, GATE_PAD), F32),
                   jax.ShapeDtypeStruct((n, GATE_PAD), F32),
                   jax.ShapeDtypeStruct((n, NA_W), BF16),
                   jax.ShapeDtypeStruct((n, NA_W), BF16),
                   jax.ShapeDtypeStruct((n, NA_W), BF16)],
        compiler_params=pltpu.CompilerParams(dimension_semantics=("parallel",),
                                             vmem_limit_bytes=VMEM_LIMIT),
        name="proj_in",
    )(x2, x2, x2, g_mix, w1, _group_mean_matrix(NA_W, NA_HEAD_DIM), gq, gk, conv_w, conv_b)


def _mem_kv_kernel(mem_ref, g_ref, w_ref, gm_ref, gk_ref, k_ref, v_ref):
    h = _rms_rows(mem_ref[...], g_ref[...]).astype(BF16)
    k = _dot(h, w_ref[:, :MEM_W])
    v_ref[...] = _dot(h, w_ref[:, MEM_W:]).astype(BF16)
    k_ref[...] = (k * lax.rsqrt(_split_dot(k * k, gm_ref[...]) + EPS) * gk_ref[...]).astype(BF16)


def _mem_kv(mem, g_mem, w_kv, gk):
    b, m, d = mem.shape
    blk = lambda width: pl.BlockSpec((None, m, width), lambda i: (i, 0, 0))
    return pl.pallas_call(
        _mem_kv_kernel,
        grid=(b,),
        in_specs=[blk(d), _const_spec((1, d)), _const_spec((d, 2 * MEM_W)),
                  _const_spec((MEM_W, MEM_W)), _const_spec((1, MEM_W))],
        out_specs=[blk(MEM_W), blk(MEM_W)],
        out_shape=[jax.ShapeDtypeStruct((b, m, MEM_W), BF16)] * 2,
        compiler_params=pltpu.CompilerParams(dimension_semantics=("parallel",),
                                             vmem_limit_bytes=VMEM_LIMIT),
        name="mem_kv",
    )(mem, g_mem, w_kv, _group_mean_matrix(MEM_W, MEM_HEAD_DIM), gk)


def _log_sigmoid(x):
    return jnp.minimum(x, 0.0) - jnp.log1p(jnp.exp(-jnp.abs(x)))


def _cummax_rows(x, reverse):
    n = x.shape[0]
    row = lax.broadcasted_iota(jnp.int32, x.shape, 0)
    sh = 1
    while sh < n:
        if reverse:
            shifted, ok = pltpu.roll(x, n - sh, 0), row < n - sh
        else:
            shifted, ok = pltpu.roll(x, sh, 0), row >= sh
        x = jnp.where(ok, jnp.maximum(x, shifted), x)
        sh *= 2
    return x


def _mlstm_gates(reverse, ig, fpre, m_st):
    L = ig.shape[0]
    last = 0 if reverse else L - 1
    r_i = lax.broadcasted_iota(jnp.int32, (L, L), 0)
    c_i = lax.broadcasted_iota(jnp.int32, (L, L), 1)
    causal = (c_i >= r_i) if reverse else (c_i <= r_i)
    lf = _log_sigmoid(fpre)
    l1 = lf.astype(BF16)
    e1 = lf - l1.astype(F32)
    l2 = e1.astype(BF16)
    l3 = (e1 - l2.astype(F32)).astype(BF16)
    bb = _dot(jnp.where(causal, 1.0, 0.0).astype(BF16), jnp.concatenate([l1, l2, l3], axis=1))
    b = bb[:, :LANES] + bb[:, LANES:2 * LANES] + bb[:, 2 * LANES:]
    a = ig - b
    big_m = jnp.maximum(m_st, _cummax_rows(a, reverse))
    a2 = a * LOG2E
    m2 = big_m * LOG2E
    return dict(
        causal=causal, last=last,
        wint=jnp.exp(m_st - big_m),
        eneg=jnp.exp(-(b + big_m)),
        m2=m2,
        a2_rows=a2.T,
        ws=jnp.exp2(a2 - m2[last:last + 1, :]),
        m_next=(b + big_m)[last:last + 1, :],
    )


def _mlstm_chains(d, g, qk, v, states):
    L = qk.shape[0]
    H = MLSTM_HEADS
    causal, last = g["causal"], g["last"]
    lane = lax.broadcasted_iota(jnp.int32, (L, LANES), 1)
    nums, new_states = [], []
    den_all = jnp.zeros((L, LANES), F32)
    for hd in range(H):
        c = d * H + hd
        pair = hd // 2
        in_head = (lane // MLSTM_QK_DIM) == (hd % 2)
        q_h = jnp.where(in_head, qk[:, pair * LANES:(pair + 1) * LANES], jnp.zeros((), BF16))
        k_p = qk[:, QK_W + pair * LANES:QK_W + (pair + 1) * LANES]
        v_ext = jnp.concatenate([v[:, hd * LANES:(hd + 1) * LANES],
                                 jnp.where(lane == c, 1.0, 0.0).astype(BF16)], axis=1)
        st = states[hd]
        decay_log = g["a2_rows"][c:c + 1, :] - g["m2"][:, c:c + 1]
        p = jnp.exp2(jnp.where(causal, decay_log, -jnp.inf))
        s_mat = _dot_nt(q_h, k_p) * p
        q_inter = q_h.astype(F32) * g["wint"][:, c:c + 1]
        lhs = jnp.concatenate([s_mat.astype(BF16), q_inter.astype(BF16)], axis=1)
        rhs = jnp.concatenate([v_ext, st.astype(BF16)], axis=0)
        num_ext = _dot(lhs, rhs)
        nums.append(num_ext[:, :LANES])
        den_all = den_all + num_ext[:, LANES:]
        kw = (k_p.astype(F32) * g["ws"][:, c:c + 1]).astype(BF16)
        new_states.append(g["wint"][last:last + 1, c:c + 1] * st + _dot_tn(kw, v_ext))
    r_all = 1.0 / jnp.maximum(jnp.abs(den_all), g["eneg"])
    h = jnp.concatenate([nums[hd] * r_all[:, d * H + hd:d * H + hd + 1] for hd in range(H)], axis=1)
    return h, new_states


def _mlstm_kernel(qkf_ref, qkb_ref, vf_ref, vb_ref, gif_ref, gib_ref, gff_ref, gfb_ref, bi_ref, bf_ref,
                  hf_ref, hb_ref, st_ref, m_ref):
    H = MLSTM_HEADS

    @pl.when(pl.program_id(1) == 0)
    def _():
        st_ref[...] = jnp.zeros_like(st_ref)
        m_ref[...] = jnp.full(m_ref.shape, -jnp.inf, F32)

    bi, bf = bi_ref[...], bf_ref[...]
    results = []
    for e in range(qkf_ref.shape[0]):
        g_f = _mlstm_gates(False, gif_ref[e] + bi, gff_ref[e] + bf, m_ref[2 * e, 0:1, :])
        g_b = _mlstm_gates(True, gib_ref[e] + bi, gfb_ref[e] + bf, m_ref[2 * e + 1, 0:1, :])
        base = 2 * H * e
        h_f, st_f = _mlstm_chains(0, g_f, qkf_ref[e], vf_ref[e], [st_ref[base + c] for c in range(H)])
        h_b, st_b = _mlstm_chains(1, g_b, qkb_ref[e], vb_ref[e], [st_ref[base + H + c] for c in range(H)])
        results.append((h_f, h_b, st_f + st_b, g_f["m_next"], g_b["m_next"]))
    for e, (h_f, h_b, sts, m_f, m_b) in enumerate(results):
        hf_ref[e] = h_f
        hb_ref[e] = h_b
        for c, st in enumerate(sts):
            st_ref[2 * H * e + c] = st
        m_ref[2 * e] = jnp.broadcast_to(m_f, m_ref.shape[1:])
        m_ref[2 * e + 1] = jnp.broadcast_to(m_b, m_ref.shape[1:])


def _mlstm(qk3, v3, gi3, gf3, bias_i, bias_f, L, nb):
    b, s, _ = qk3.shape
    nc = s // L
    fwd = lambda bi, c: (bi, c, 0)
    bwd = lambda bi, c: (bi, nc - 1 - c, 0)
    blk = lambda width, pos: pl.BlockSpec((nb, L, width), pos)
    return pl.pallas_call(
        _mlstm_kernel,
        grid=(b // nb, nc),
        in_specs=[blk(2 * QK_W, fwd), blk(2 * QK_W, bwd), blk(V_W, fwd), blk(V_W, bwd),
                  blk(GATE_PAD, fwd), blk(GATE_PAD, bwd), blk(GATE_PAD, fwd), blk(GATE_PAD, bwd),
                  _const_spec((1, GATE_PAD)), _const_spec((1, GATE_PAD))],
        out_specs=[blk(V_W, fwd), blk(V_W, bwd)],
        out_shape=[jax.ShapeDtypeStruct((b, s, V_W), F32)] * 2,
        scratch_shapes=[pltpu.VMEM((nb * 2 * MLSTM_HEADS, LANES, 2 * LANES), F32),
                        pltpu.VMEM((nb * 2, SUBLANES, LANES), F32)],
        compiler_params=pltpu.CompilerParams(dimension_semantics=("parallel", "arbitrary"),
                                             vmem_limit_bytes=VMEM_LIMIT),
        name="mlstm",
    )(qk3, qk3, v3, v3, gi3, gi3, gf3, gf3, bias_i, bias_f)


def _stack_heads(q, n_heads, head_dim):
    lane = lax.broadcasted_iota(jnp.int32, q.shape, 1)
    zero = jnp.zeros_like(q)
    return jnp.concatenate([jnp.where(lane // head_dim == h, q, zero) for h in range(n_heads)], axis=0)


def _unstack_heads(o, n_heads, head_dim):
    t = o.shape[0] // n_heads
    lane = lax.broadcasted_iota(jnp.int32, (t, o.shape[1]), 1)
    acc = jnp.zeros((t, o.shape[1]), o.dtype)
    for h in range(n_heads):
        acc = jnp.where(lane // head_dim == h, o[h * t:(h + 1) * t, :], acc)
    return acc


def _masked_attention(q_stacked, k, v, bias):
    sc = _dot_nt(q_stacked, k)
    if bias is not None:
        sc = sc + bias
    p = jnp.exp2(sc - jnp.max(sc, axis=-1, keepdims=True))
    o = _dot(p.astype(BF16), v)
    return o / jnp.sum(p, axis=-1, keepdims=True)


def _natten_kernel(q_ref, k_ref, v_ref, bias_ref, o_ref, *, rows, rows_per_step):
    for j in range(rows_per_step):
        r = pl.program_id(1) * rows_per_step + j
        rs = jnp.clip(r - NA_WIN_ROWS // 2, 0, rows - NA_WIN_ROWS)
        start = pl.multiple_of(rs * GRID_W, GRID_W)
        k_win = k_ref[pl.ds(start, NA_WIN_ROWS * GRID_W), :]
        v_win = v_ref[pl.ds(start, NA_WIN_ROWS * GRID_W), :]
        q = q_ref[j * GRID_W:(j + 1) * GRID_W, :]
        o = _masked_attention(_stack_heads(q, NA_HEADS, NA_HEAD_DIM), k_win, v_win, bias_ref[r - rs])
        o_ref[j * GRID_W:(j + 1) * GRID_W, :] = _unstack_heads(o, NA_HEADS, NA_HEAD_DIM).astype(o_ref.dtype)


def _na_bias_kernel(rpb_ref, onehot_ref, mask_ref, o_ref):
    r = rpb_ref[...]
    r1 = r.astype(BF16)
    e1 = r - r1.astype(F32)
    r2 = e1.astype(BF16)
    r3 = (e1 - r2.astype(F32)).astype(BF16)
    oh = onehot_ref[...]
    o_ref[...] = (_dot(r1, oh) + _dot(r2, oh) + _dot(r3, oh)) * LOG2E + mask_ref[...]


def _na_bias_table(rpb):
    n_ro, n_co = 2 * NA_WIN_ROWS - 1, 2 * NA_WIN_COLS - 1
    c = np.arange(GRID_W)[:, None]
    kc = np.arange(GRID_W)[None, :]
    cs = np.clip(c - NA_WIN_COLS // 2, 0, GRID_W - NA_WIN_COLS)
    valid = (kc >= cs) & (kc < cs + NA_WIN_COLS)
    col_off = kc - c + NA_WIN_COLS - 1
    onehot = (np.arange(LANES)[:, None, None] == col_off[None]) & valid[None]
    onehot = jnp.asarray(onehot.reshape(LANES, GRID_W * GRID_W), dtype=BF16)
    mask = jnp.asarray(np.where(valid, 0.0, NEG_BIG).reshape(1, GRID_W * GRID_W), dtype=F32)
    rp = jnp.pad(rpb.astype(F32).reshape(NA_HEADS * n_ro, n_co), ((0, 0), (0, LANES - n_co)))
    toep = pl.pallas_call(
        _na_bias_kernel,
        out_shape=jax.ShapeDtypeStruct((NA_HEADS * n_ro, GRID_W * GRID_W), F32),
        name="na_bias",
    )(rp, onehot, mask)
    toep = toep.reshape(NA_HEADS, n_ro, GRID_W, GRID_W).transpose(0, 2, 1, 3)
    toep = toep.reshape(NA_HEADS * GRID_W, n_ro * GRID_W)
    wk = NA_WIN_ROWS * GRID_W
    return jnp.stack([toep[:, (NA_WIN_ROWS - 1 - var) * GRID_W:(NA_WIN_ROWS - 1 - var) * GRID_W + wk]
                      for var in range(NA_WIN_ROWS)])


def _natten(qn, kn, vn, bias_tab, rows_per_step):
    b, s, _ = qn.shape
    rows = s // GRID_W
    full = pl.BlockSpec((None, s, NA_W), lambda bi, r: (bi, 0, 0))
    tile = pl.BlockSpec((None, rows_per_step * GRID_W, NA_W), lambda bi, r: (bi, r, 0))
    return pl.pallas_call(
        functools.partial(_natten_kernel, rows=rows, rows_per_step=rows_per_step),
        grid=(b, rows // rows_per_step),
        in_specs=[tile, full, full, _const_spec(bias_tab.shape)],
        out_specs=tile,
        out_shape=jax.ShapeDtypeStruct((b, s, NA_W), BF16),
        compiler_params=pltpu.CompilerParams(dimension_semantics=("parallel", "arbitrary"),
                                             vmem_limit_bytes=VMEM_LIMIT),
        name="natten",
    )(qn, kn, vn, bias_tab)


def _merge_kernel(x_ref, hf_ref, hb_ref, hna_ref, km_ref, vm_ref,
                  g_ref, w2_ref, gml_ref, gmq_ref, gm64_ref,
                  wpm_ref, wpn_ref, wpx_ref, wout_ref, o_ref):
    x = x_ref[...]
    d = x.shape[1]
    h = _rms_rows(x, g_ref[...]).astype(BF16)
    o_pre = _dot(h, w2_ref[:, :V_W])
    q_mem = _dot(h, w2_ref[:, V_W:V_W + MEM_W])
    g0 = V_W + MEM_W

    hm = hf_ref[...] + hb_ref[...]
    gml = gml_ref[...]
    parts = []
    for hd in range(MLSTM_HEADS):
        sl = slice(hd * MLSTM_V_DIM, (hd + 1) * MLSTM_V_DIM)
        parts.append(_rms_rows(hm[:, sl], gml[:, sl]))
    hm = (jnp.concatenate(parts, axis=1) * jax.nn.sigmoid(o_pre)).astype(BF16)
    y = jax.nn.sigmoid(_dot(h, w2_ref[:, g0:g0 + d])) * _dot(hm, wpm_ref[...])

    y = y + jax.nn.sigmoid(_dot(h, w2_ref[:, g0 + d:g0 + 2 * d])) * _dot(hna_ref[...], wpn_ref[...])

    qn = q_mem * lax.rsqrt(_split_dot(q_mem * q_mem, gm64_ref[...]) + EPS) * gmq_ref[...]
    qn = (qn * (MEM_HEAD_DIM ** -0.5 * LOG2E)).astype(BF16)
    att = _masked_attention(_stack_heads(qn, MEM_HEADS, MEM_HEAD_DIM), km_ref[...], vm_ref[...], None)
    h_mem = _unstack_heads(att, MEM_HEADS, MEM_HEAD_DIM).astype(BF16)
    y = y + jax.nn.sigmoid(_dot(h, w2_ref[:, g0 + 2 * d:g0 + 3 * d])) * _dot(h_mem, wpx_ref[...])

    o_ref[...] = x + _dot(y.astype(BF16), wout_ref[...])


def _merge(x2, hf, hb, hna, k_mem, v_mem, g_mix, w2, g_mlstm, gmq, wpm, wpn, wpx, wout, tm, s):
    n, d = x2.shape
    m = k_mem.shape[1]
    per_b = s // tm
    row = lambda width: pl.BlockSpec((tm, width), lambda i: (i, 0))
    memblk = pl.BlockSpec((None, m, MEM_W), lambda i: (i // per_b, 0, 0))
    return pl.pallas_call(
        _merge_kernel,
        grid=(n // tm,),
        in_specs=[row(d), row(V_W), row(V_W), row(NA_W), memblk, memblk,
                  _const_spec((1, d)), _const_spec(w2.shape), _const_spec((1, V_W)),
                  _const_spec((1, MEM_W)), _const_spec((MEM_W, MEM_W)),
                  _const_spec(wpm.shape), _const_spec(wpn.shape), _const_spec(wpx.shape),
                  _const_spec(wout.shape)],
        out_specs=row(d),
        out_shape=jax.ShapeDtypeStruct((n, d), F32),
        compiler_params=pltpu.CompilerParams(dimension_semantics=("parallel",),
                                             vmem_limit_bytes=VMEM_LIMIT),
        name="merge",
    )(x2, hf, hb, hna, k_mem, v_mem, g_mix, w2, g_mlstm, gmq,
      _group_mean_matrix(MEM_W, MEM_HEAD_DIM), wpm, wpn, wpx, wout)


def _ffn_kernel(x_ref, g_ref, wu_ref, wd_ref, o_ref, *, n_chunks):
    x = x_ref[...]
    h = _rms_rows(x, g_ref[...]).astype(BF16)
    ck = wu_ref.shape[1] // n_chunks
    acc = x
    for j in range(n_chunks):
        u = jnp.maximum(_dot(h, wu_ref[:, j * ck:(j + 1) * ck]), 0.0)
        acc = acc + _dot((u * u).astype(BF16), wd_ref[j * ck:(j + 1) * ck, :])
    o_ref[...] = acc


def _ffn(x2, g_ffn, w_up, w_down, tm):
    n, d = x2.shape
    row = pl.BlockSpec((tm, d), lambda i: (i, 0))
    return pl.pallas_call(
        functools.partial(_ffn_kernel, n_chunks=w_up.shape[1] // d),
        grid=(n // tm,),
        in_specs=[row, _const_spec((1, d)), _const_spec(w_up.shape), _const_spec(w_down.shape)],
        out_specs=row,
        out_shape=jax.ShapeDtypeStruct((n, d), F32),
        compiler_params=pltpu.CompilerParams(dimension_semantics=("parallel",),
                                             vmem_limit_bytes=VMEM_LIMIT),
        name="ffn",
    )(x2, g_ffn, w_up, w_down)


def _layer(x, mem, g_mix, w_in, conv_w, conv_b, b_igate, b_fgate, g_mlstm, w_proj_mlstm,
           g_na_q, g_na_k, rpb, w_proj_na, g_mem, w_mem_kv, g_mem_q, g_mem_k,
           w_proj_mem, w_out, g_ffn, w_up, w_down):
    bsz, s, d = x.shape
    n = bsz * s
    tm = min(512, s)
    x2 = x.reshape(n, d)

    o_qk, o_v = 0, 2 * QK_W
    o_o = o_v + V_W
    o_i = o_o + V_W
    o_f = o_i + 2 * MLSTM_HEADS
    o_na = o_f + 2 * MLSTM_HEADS
    o_qm = o_na + 3 * NA_W
    o_g = o_qm + MEM_W
    gate_pad = ((0, 0), (0, GATE_PAD - N_GATES // 2))
    w_gi = jnp.pad(w_in[:, o_i:o_f], gate_pad)
    w_gf = jnp.pad(w_in[:, o_f:o_na], gate_pad)
    bias_i = jnp.pad(b_igate.reshape(1, -1).astype(F32), gate_pad)
    bias_f = jnp.pad(b_fgate.reshape(1, -1).astype(F32), gate_pad)
    w1 = jnp.concatenate([w_in[:, o_qk:o_o], w_gi, w_gf, w_in[:, o_na:o_qm]], axis=1).astype(BF16)
    w2 = jnp.concatenate([w_in[:, o_o:o_i], w_in[:, o_qm:o_g], w_in[:, o_g:]], axis=1).astype(BF16)
    row = lambda a: a.reshape(1, -1).astype(F32)

    qk, v, gi, gf, qn, kn, vn = _proj_in(x2, row(g_mix), w1, row(jnp.tile(g_na_q, NA_HEADS)),
                                         row(jnp.tile(g_na_k, NA_HEADS)), conv_w.astype(F32),
                                         row(conv_b), tm, s)
    k_mem, v_mem = _mem_kv(mem, row(g_mem), w_mem_kv.astype(BF16), row(jnp.tile(g_mem_k, MEM_HEADS)))

    L = min(MLSTM_CHUNK, s)
    hf, hb = _mlstm(qk.reshape(bsz, s, -1), v.reshape(bsz, s, -1), gi.reshape(bsz, s, -1),
                    gf.reshape(bsz, s, -1), bias_i, bias_f, L, 2 if bsz % 2 == 0 else 1)
    hna = _natten(qn.reshape(bsz, s, -1), kn.reshape(bsz, s, -1), vn.reshape(bsz, s, -1),
                  _na_bias_table(rpb), 4 if (s // GRID_W) % 4 == 0 else 1)

    x1 = _merge(x2, hf.reshape(n, -1), hb.reshape(n, -1), hna.reshape(n, -1), k_mem, v_mem,
                row(g_mix), w2, row(g_mlstm), row(jnp.tile(g_mem_q, MEM_HEADS)),
                w_proj_mlstm.astype(BF16), w_proj_na.astype(BF16), w_proj_mem.astype(BF16),
                w_out.astype(BF16), tm, s)
    out = _ffn(x1, row(g_ffn), w_up.astype(BF16), w_down.astype(BF16), tm)
    return out.reshape(bsz, s, d)


def kernel(x, mem, g_mix, w_in, conv_w, conv_b, b_igate, b_fgate, g_mlstm, w_proj_mlstm,
           g_na_q, g_na_k, rpb, w_proj_na, g_mem, w_mem_kv, g_mem_q, g_mem_k,
           w_proj_mem, w_out, g_ffn, w_up, w_down):
    for l in range(g_mix.shape[0]):
        x = _layer(x, mem, g_mix[l], w_in[l], conv_w[l], conv_b[l], b_igate[l], b_fgate[l],
                   g_mlstm[l], w_proj_mlstm[l], g_na_q[l], g_na_k[l], rpb[l], w_proj_na[l],
                   g_mem[l], w_mem_kv[l], g_mem_q[l], g_mem_k[l], w_proj_mem[l], w_out[l],
                   g_ffn[l], w_up[l], w_down[l])
    return x
```

```python
import functools

import numpy as np
import jax
import jax.numpy as jnp
from jax import lax
from jax.experimental import pallas as pl
from jax.experimental.pallas import tpu as pltpu

GRID_W = 64
MLSTM_HEADS = 4
MLSTM_QK_DIM = 64
MLSTM_V_DIM = 128
MLSTM_CONV = 5
NA_HEADS = 8
NA_HEAD_DIM = 32
NA_WIN_ROWS = 8
NA_WIN_COLS = 16
MEM_HEADS = 4
MEM_HEAD_DIM = 64
N_BRANCH = 3
EPS = 1e-6

QK_W = MLSTM_HEADS * MLSTM_QK_DIM
V_W = MLSTM_HEADS * MLSTM_V_DIM
NA_W = NA_HEADS * NA_HEAD_DIM
MEM_W = MEM_HEADS * MEM_HEAD_DIM
N_GATES = 4 * MLSTM_HEADS
LANES = 128
SUBLANES = 8
HALO = SUBLANES
GATE_PAD = LANES
NEG_BIG = -1e30
LOG2E = 1.4426950408889634

MLSTM_CHUNK = 256
VMEM_LIMIT = 48 * 1024 * 1024

BF16 = jnp.bfloat16
F32 = jnp.float32


def _dot(a, b):
    return jnp.dot(a, b, preferred_element_type=F32)


def _dot_nt(a, b):
    return lax.dot_general(a, b, (((1,), (1,)), ((), ())), preferred_element_type=F32)


def _dot_tn(a, b):
    return lax.dot_general(a, b, (((0,), (0,)), ((), ())), preferred_element_type=F32)


def _split_dot(a, m_bf16):
    hi = a.astype(BF16)
    lo = (a - hi.astype(F32)).astype(BF16)
    return _dot(hi, m_bf16) + _dot(lo, m_bf16)


def _rms_rows(x, g):
    ms = jnp.mean(x * x, axis=-1, keepdims=True)
    return x * lax.rsqrt(ms + EPS) * g


def _group_mean_matrix(width, group):
    idx = np.arange(width) // group
    return jnp.asarray((idx[:, None] == idx[None, :]).astype(np.float32) / group, dtype=BF16)


def _const_spec(shape):
    nd = len(shape)
    return pl.BlockSpec(shape, lambda *_: (0,) * nd, pipeline_mode=pl.Buffered(1))


def _proj_in_kernel(x_ref, xp_ref, xn_ref, g_ref, w_ref, gm_ref, gq_ref, gk_ref, cw_ref, cb_ref,
                    qk_ref, v_ref, gi_ref, gf_ref, qn_ref, kn_ref, vn_ref, *, tiles_per_seq):
    i = pl.program_id(0)
    tm = x_ref.shape[0]
    g = g_ref[...]
    h = _rms_rows(x_ref[...], g).astype(BF16)
    pos = i % tiles_per_seq
    hp = jnp.where(pos > 0, _rms_rows(xp_ref[...], g), 0.0).astype(BF16)
    hn = jnp.where(pos < tiles_per_seq - 1, _rms_rows(xn_ref[...], g), 0.0).astype(BF16)
    o = 0
    z = _dot(jnp.concatenate([hp, h, hn], axis=0), w_ref[:, o:o + 2 * QK_W]); o += 2 * QK_W
    cw = cw_ref[...]
    acc = jnp.zeros((tm, 2 * QK_W), F32) + cb_ref[...]
    for j in range(MLSTM_CONV):
        d = j - MLSTM_CONV // 2
        tap = z if d == 0 else pltpu.roll(z, (-d) % z.shape[0], 0)
        acc = acc + tap[HALO:HALO + tm, :] * cw[j:j + 1, :]
    act = acc * jax.nn.sigmoid(acc)
    lane = lax.broadcasted_iota(jnp.int32, act.shape, 1)
    qk_ref[...] = jnp.where(lane < QK_W, act * (MLSTM_QK_DIM ** -0.5), act).astype(BF16)
    v_ref[...] = _dot(h, w_ref[:, o:o + V_W]).astype(BF16); o += V_W
    gi_ref[...] = _dot(h, w_ref[:, o:o + GATE_PAD]); o += GATE_PAD
    gf_ref[...] = _dot(h, w_ref[:, o:o + GATE_PAD]); o += GATE_PAD
    q = _dot(h, w_ref[:, o:o + NA_W]); o += NA_W
    k = _dot(h, w_ref[:, o:o + NA_W]); o += NA_W
    vn_ref[...] = _dot(h, w_ref[:, o:o + NA_W]).astype(BF16)
    gm = gm_ref[...]
    qn = q * lax.rsqrt(_split_dot(q * q, gm) + EPS) * gq_ref[...]
    qn_ref[...] = (qn * (NA_HEAD_DIM ** -0.5 * LOG2E)).astype(BF16)
    kn_ref[...] = (k * lax.rsqrt(_split_dot(k * k, gm) + EPS) * gk_ref[...]).astype(BF16)


def _proj_in(x2, g_mix, w1, gq, gk, conv_w, conv_b, tm, s):
    n, d = x2.shape
    w_cols = w1.shape[1]
    hb = tm // HALO
    row = lambda width: pl.BlockSpec((tm, width), lambda i: (i, 0))
    prev = pl.BlockSpec((HALO, d), lambda i: (jnp.maximum(i * hb - 1, 0), 0))
    nxt = pl.BlockSpec((HALO, d), lambda i: (jnp.minimum((i + 1) * hb, n // HALO - 1), 0))
    return pl.pallas_call(
        functools.partial(_proj_in_kernel, tiles_per_seq=s // tm),
        grid=(n // tm,),
        in_specs=[row(d), prev, nxt, _const_spec((1, d)), _const_spec((d, w_cols)),
                  _const_spec((NA_W, NA_W)), _const_spec((1, NA_W)), _const_spec((1, NA_W)),
                  _const_spec((MLSTM_CONV, 2 * QK_W)), _const_spec((1, 2 * QK_W))],
        out_specs=[row(2 * QK_W), row(V_W), row(GATE_PAD), row(GATE_PAD), row(NA_W), row(NA_W), row(NA_W)],
        out_shape=[jax.ShapeDtypeStruct((n, 2 * QK_W), BF16),
                   jax.ShapeDtypeStruct((n, V_W), BF16),
                   jax.ShapeDtypeStruct((n, GATE_PAD), F32),
                   jax.ShapeDtypeStruct((n, GATE_PAD), F32),
                   jax.ShapeDtypeStruct((n, NA_W), BF16),
                   jax.ShapeDtypeStruct((n, NA_W), BF16),
                   jax.ShapeDtypeStruct((n, NA_W), BF16)],
        compiler_params=pltpu.CompilerParams(dimension_semantics=("parallel",),
                                             vmem_limit_bytes=VMEM_LIMIT),
        name="proj_in",
    )(x2, x2, x2, g_mix, w1, _group_mean_matrix(NA_W, NA_HEAD_DIM), gq, gk, conv_w, conv_b)


def _mem_kv_kernel(mem_ref, g_ref, w_ref, gm_ref, gk_ref, k_ref, v_ref):
    h = _rms_rows(mem_ref[...], g_ref[...]).astype(BF16)
    k = _dot(h, w_ref[:, :MEM_W])
    v_ref[...] = _dot(h, w_ref[:, MEM_W:]).astype(BF16)
    k_ref[...] = (k * lax.rsqrt(_split_dot(k * k, gm_ref[...]) + EPS) * gk_ref[...]).astype(BF16)


def _mem_kv(mem, g_mem, w_kv, gk):
    b, m, d = mem.shape
    blk = lambda width: pl.BlockSpec((None, m, width), lambda i: (i, 0, 0))
    return pl.pallas_call(
        _mem_kv_kernel,
        grid=(b,),
        in_specs=[blk(d), _const_spec((1, d)), _const_spec((d, 2 * MEM_W)),
                  _const_spec((MEM_W, MEM_W)), _const_spec((1, MEM_W))],
        out_specs=[blk(MEM_W), blk(MEM_W)],
        out_shape=[jax.ShapeDtypeStruct((b, m, MEM_W), BF16)] * 2,
        compiler_params=pltpu.CompilerParams(dimension_semantics=("parallel",),
                                             vmem_limit_bytes=VMEM_LIMIT),
        name="mem_kv",
    )(mem, g_mem, w_kv, _group_mean_matrix(MEM_W, MEM_HEAD_DIM), gk)


def _log_sigmoid(x):
    return jnp.minimum(x, 0.0) - jnp.log1p(jnp.exp(-jnp.abs(x)))


def _cummax_rows(x, reverse):
    n = x.shape[0]
    row = lax.broadcasted_iota(jnp.int32, x.shape, 0)
    sh = 1
    while sh < n:
        if reverse:
            shifted, ok = pltpu.roll(x, n - sh, 0), row < n - sh
        else:
            shifted, ok = pltpu.roll(x, sh, 0), row >= sh
        x = jnp.where(ok, jnp.maximum(x, shifted), x)
        sh *= 2
    return x


def _mlstm_gates(reverse, ig, fpre, m_st):
    L = ig.shape[0]
    last = 0 if reverse else L - 1
    r_i = lax.broadcasted_iota(jnp.int32, (L, L), 0)
    c_i = lax.broadcasted_iota(jnp.int32, (L, L), 1)
    causal = (c_i >= r_i) if reverse else (c_i <= r_i)
    lf = _log_sigmoid(fpre)
    l1 = lf.astype(BF16)
    e1 = lf - l1.astype(F32)
    l2 = e1.astype(BF16)
    l3 = (e1 - l2.astype(F32)).astype(BF16)
    bb = _dot(jnp.where(causal, 1.0, 0.0).astype(BF16), jnp.concatenate([l1, l2, l3], axis=1))
    b = bb[:, :LANES] + bb[:, LANES:2 * LANES] + bb[:, 2 * LANES:]
    a = ig - b
    big_m = jnp.maximum(m_st, _cummax_rows(a, reverse))
    a2 = a * LOG2E
    m2 = big_m * LOG2E
    return dict(
        causal=causal, last=last,
        wint=jnp.exp(m_st - big_m),
        eneg=jnp.exp(-(b + big_m)),
        m2=m2,
        a2_rows=a2.T,
        ws=jnp.exp2(a2 - m2[last:last + 1, :]),
        m_next=(b + big_m)[last:last + 1, :],
    )


def _mlstm_chains(d, g, qk, v, states):
    L = qk.shape[0]
    H = MLSTM_HEADS
    causal, last = g["causal"], g["last"]
    lane = lax.broadcasted_iota(jnp.int32, (L, LANES), 1)
    nums, new_states = [], []
    den_all = jnp.zeros((L, LANES), F32)
    for hd in range(H):
        c = d * H + hd
        pair = hd // 2
        in_head = (lane // MLSTM_QK_DIM) == (hd % 2)
        q_h = jnp.where(in_head, qk[:, pair * LANES:(pair + 1) * LANES], jnp.zeros((), BF16))
        k_p = qk[:, QK_W + pair * LANES:QK_W + (pair + 1) * LANES]
        v_ext = jnp.concatenate([v[:, hd * LANES:(hd + 1) * LANES],
                                 jnp.where(lane == c, 1.0, 0.0).astype(BF16)], axis=1)
        st = states[hd]
        decay_log = g["a2_rows"][c:c + 1, :] - g["m2"][:, c:c + 1]
        p = jnp.exp2(jnp.where(causal, decay_log, -jnp.inf))
        s_mat = _dot_nt(q_h, k_p) * p
        q_inter = q_h.astype(F32) * g["wint"][:, c:c + 1]
        lhs = jnp.concatenate([s_mat.astype(BF16), q_inter.astype(BF16)], axis=1)
        rhs = jnp.concatenate([v_ext, st.astype(BF16)], axis=0)
        num_ext = _dot(lhs, rhs)
        nums.append(num_ext[:, :LANES])
        den_all = den_all + num_ext[:, LANES:]
        kw = (k_p.astype(F32) * g["ws"][:, c:c + 1]).astype(BF16)
        new_states.append(g["wint"][last:last + 1, c:c + 1] * st + _dot_tn(kw, v_ext))
    r_all = 1.0 / jnp.maximum(jnp.abs(den_all), g["eneg"])
    h = jnp.concatenate([nums[hd] * r_all[:, d * H + hd:d * H + hd + 1] for hd in range(H)], axis=1)
    return h, new_states


def _mlstm_kernel(qkf_ref, qkb_ref, vf_ref, vb_ref, gif_ref, gib_ref, gff_ref, gfb_ref, bi_ref, bf_ref,
                  hf_ref, hb_ref, st_ref, m_ref):
    H = MLSTM_HEADS

    @pl.when(pl.program_id(1) == 0)
    def _():
        st_ref[...] = jnp.zeros_like(st_ref)
        m_ref[...] = jnp.full(m_ref.shape, -jnp.inf, F32)

    bi, bf = bi_ref[...], bf_ref[...]
    results = []
    for e in range(qkf_ref.shape[0]):
        g_f = _mlstm_gates(False, gif_ref[e] + bi, gff_ref[e] + bf, m_ref[2 * e, 0:1, :])
        g_b = _mlstm_gates(True, gib_ref[e] + bi, gfb_ref[e] + bf, m_ref[2 * e + 1, 0:1, :])
        base = 2 * H * e
        h_f, st_f = _mlstm_chains(0, g_f, qkf_ref[e], vf_ref[e], [st_ref[base + c] for c in range(H)])
        h_b, st_b = _mlstm_chains(1, g_b, qkb_ref[e], vb_ref[e], [st_ref[base + H + c] for c in range(H)])
        results.append((h_f, h_b, st_f + st_b, g_f["m_next"], g_b["m_next"]))
    for e, (h_f, h_b, sts, m_f, m_b) in enumerate(results):
        hf_ref[e] = h_f
        hb_ref[e] = h_b
        for c, st in enumerate(sts):
            st_ref[2 * H * e + c] = st
        m_ref[2 * e] = jnp.broadcast_to(m_f, m_ref.shape[1:])
        m_ref[2 * e + 1] = jnp.broadcast_to(m_b, m_ref.shape[1:])


def _mlstm(qk3, v3, gi3, gf3, bias_i, bias_f, L, nb):
    b, s, _ = qk3.shape
    nc = s // L
    fwd = lambda bi, c: (bi, c, 0)
    bwd = lambda bi, c: (bi, nc - 1 - c, 0)
    blk = lambda width, pos: pl.BlockSpec((nb, L, width), pos)
    return pl.pallas_call(
        _mlstm_kernel,
        grid=(b // nb, nc),
        in_specs=[blk(2 * QK_W, fwd), blk(2 * QK_W, bwd), blk(V_W, fwd), blk(V_W, bwd),
                  blk(GATE_PAD, fwd), blk(GATE_PAD, bwd), blk(GATE_PAD, fwd), blk(GATE_PAD, bwd),
                  _const_spec((1, GATE_PAD)), _const_spec((1, GATE_PAD))],
        out_specs=[blk(V_W, fwd), blk(V_W, bwd)],
        out_shape=[jax.ShapeDtypeStruct((b, s, V_W), F32)] * 2,
        scratch_shapes=[pltpu.VMEM((nb * 2 * MLSTM_HEADS, LANES, 2 * LANES), F32),
                        pltpu.VMEM((nb * 2, SUBLANES, LANES), F32)],
        compiler_params=pltpu.CompilerParams(dimension_semantics=("parallel", "arbitrary"),
                                             vmem_limit_bytes=VMEM_LIMIT),
        name="mlstm",
    )(qk3, qk3, v3, v3, gi3, gi3, gf3, gf3, bias_i, bias_f)


def _stack_heads(q, n_heads, head_dim):
    lane = lax.broadcasted_iota(jnp.int32, q.shape, 1)
    zero = jnp.zeros_like(q)
    return jnp.concatenate([jnp.where(lane // head_dim == h, q, zero) for h in range(n_heads)], axis=0)


def _unstack_heads(o, n_heads, head_dim):
    t = o.shape[0] // n_heads
    lane = lax.broadcasted_iota(jnp.int32, (t, o.shape[1]), 1)
    acc = jnp.zeros((t, o.shape[1]), o.dtype)
    for h in range(n_heads):
        acc = jnp.where(lane // head_dim == h, o[h * t:(h + 1) * t, :], acc)
    return acc


def _masked_attention(q_stacked, k, v, bias):
    sc = _dot_nt(q_stacked, k)
    if bias is not None:
        sc = sc + bias
    p = jnp.exp2(sc - jnp.max(sc, axis=-1, keepdims=True))
    o = _dot(p.astype(BF16), v)
    return o / jnp.sum(p, axis=-1, keepdims=True)


def _natten_kernel(q_ref, k_ref, v_ref, bias_ref, o_ref, *, rows, rows_per_step):
    for j in range(rows_per_step):
        r = pl.program_id(1) * rows_per_step + j
        rs = jnp.clip(r - NA_WIN_ROWS // 2, 0, rows - NA_WIN_ROWS)
        start = pl.multiple_of(rs * GRID_W, GRID_W)
        k_win = k_ref[pl.ds(start, NA_WIN_ROWS * GRID_W), :]
        v_win = v_ref[pl.ds(start, NA_WIN_ROWS * GRID_W), :]
        q = q_ref[j * GRID_W:(j + 1) * GRID_W, :]
        off = NA_WIN_ROWS - 1 - (r - rs)
        lane0 = pl.multiple_of((off // 2) * LANES, LANES)
        bias = bias_ref[off % 2, :, pl.ds(lane0, NA_WIN_ROWS * GRID_W)]
        o = _masked_attention(_stack_heads(q, NA_HEADS, NA_HEAD_DIM), k_win, v_win, bias)
        o_ref[j * GRID_W:(j + 1) * GRID_W, :] = _unstack_heads(o, NA_HEADS, NA_HEAD_DIM).astype(o_ref.dtype)


def _na_bias_kernel(rpb_ref, onehot_ref, mask_ref, o_ref):
    r = rpb_ref[...]
    r1 = r.astype(BF16)
    e1 = r - r1.astype(F32)
    r2 = e1.astype(BF16)
    r3 = (e1 - r2.astype(F32)).astype(BF16)
    oh = onehot_ref[...]
    o_ref[...] = (_dot(r1, oh) + _dot(r2, oh) + _dot(r3, oh)) * LOG2E + mask_ref[...]


def _na_bias_table(rpb):
    n_ro, n_co = 2 * NA_WIN_ROWS - 1, 2 * NA_WIN_COLS - 1
    c = np.arange(GRID_W)[:, None]
    kc = np.arange(GRID_W)[None, :]
    cs = np.clip(c - NA_WIN_COLS // 2, 0, GRID_W - NA_WIN_COLS)
    valid = (kc >= cs) & (kc < cs + NA_WIN_COLS)
    col_off = kc - c + NA_WIN_COLS - 1
    onehot = (np.arange(LANES)[:, None, None] == col_off[None]) & valid[None]
    onehot = jnp.asarray(onehot.reshape(LANES, GRID_W * GRID_W), dtype=BF16)
    mask = jnp.asarray(np.where(valid, 0.0, NEG_BIG).reshape(1, GRID_W * GRID_W), dtype=F32)
    rp = jnp.pad(rpb.astype(F32).reshape(NA_HEADS * n_ro, n_co), ((0, 0), (0, LANES - n_co)))
    toep = pl.pallas_call(
        _na_bias_kernel,
        out_shape=jax.ShapeDtypeStruct((NA_HEADS * n_ro, GRID_W * GRID_W), F32),
        name="na_bias",
    )(rp, onehot, mask)
    toep = toep.reshape(NA_HEADS, n_ro, GRID_W, GRID_W).transpose(0, 2, 1, 3)
    toep = toep.reshape(NA_HEADS * GRID_W, n_ro * GRID_W)
    width = (n_ro - 1) * GRID_W
    return jnp.stack([toep[:, :width], toep[:, GRID_W:]])


def _natten(qn, kn, vn, bias_tab, rows_per_step):
    b, s, _ = qn.shape
    rows = s // GRID_W
    full = pl.BlockSpec((None, s, NA_W), lambda bi, r: (bi, 0, 0))
    tile = pl.BlockSpec((None, rows_per_step * GRID_W, NA_W), lambda bi, r: (bi, r, 0))
    return pl.pallas_call(
        functools.partial(_natten_kernel, rows=rows, rows_per_step=rows_per_step),
        grid=(b, rows // rows_per_step),
        in_specs=[tile, full, full, _const_spec(bias_tab.shape)],
        out_specs=tile,
        out_shape=jax.ShapeDtypeStruct((b, s, NA_W), BF16),
        compiler_params=pltpu.CompilerParams(dimension_semantics=("parallel", "arbitrary"),
                                             vmem_limit_bytes=VMEM_LIMIT),
        name="natten",
    )(qn, kn, vn, bias_tab)


def _merge_kernel(x_ref, hf_ref, hb_ref, hna_ref, km_ref, vm_ref,
                  g_ref, w2_ref, gml_ref, gmq_ref, gm64_ref,
                  wpm_ref, wpn_ref, wpx_ref, wout_ref, o_ref):
    x = x_ref[...]
    d = x.shape[1]
    h = _rms_rows(x, g_ref[...]).astype(BF16)
    o_pre = _dot(h, w2_ref[:, :V_W])
    q_mem = _dot(h, w2_ref[:, V_W:V_W + MEM_W])
    g0 = V_W + MEM_W

    hm = hf_ref[...] + hb_ref[...]
    gml = gml_ref[...]
    parts = []
    for hd in range(MLSTM_HEADS):
        sl = slice(hd * MLSTM_V_DIM, (hd + 1) * MLSTM_V_DIM)
        parts.append(_rms_rows(hm[:, sl], gml[:, sl]))
    hm = (jnp.concatenate(parts, axis=1) * jax.nn.sigmoid(o_pre)).astype(BF16)
    y = jax.nn.sigmoid(_dot(h, w2_ref[:, g0:g0 + d])) * _dot(hm, wpm_ref[...])

    y = y + jax.nn.sigmoid(_dot(h, w2_ref[:, g0 + d:g0 + 2 * d])) * _dot(hna_ref[...], wpn_ref[...])

    qn = q_mem * lax.rsqrt(_split_dot(q_mem * q_mem, gm64_ref[...]) + EPS) * gmq_ref[...]
    qn = (qn * (MEM_HEAD_DIM ** -0.5 * LOG2E)).astype(BF16)
    att = _masked_attention(_stack_heads(qn, MEM_HEADS, MEM_HEAD_DIM), km_ref[...], vm_ref[...], None)
    h_mem = _unstack_heads(att, MEM_HEADS, MEM_HEAD_DIM).astype(BF16)
    y = y + jax.nn.sigmoid(_dot(h, w2_ref[:, g0 + 2 * d:g0 + 3 * d])) * _dot(h_mem, wpx_ref[...])

    o_ref[...] = x + _dot(y.astype(BF16), wout_ref[...])


def _merge(x2, hf, hb, hna, k_mem, v_mem, g_mix, w2, g_mlstm, gmq, wpm, wpn, wpx, wout, tm, s):
    n, d = x2.shape
    m = k_mem.shape[1]
    per_b = s // tm
    row = lambda width: pl.BlockSpec((tm, width), lambda i: (i, 0))
    memblk = pl.BlockSpec((None, m, MEM_W), lambda i: (i // per_b, 0, 0))
    return pl.pallas_call(
        _merge_kernel,
        grid=(n // tm,),
        in_specs=[row(d), row(V_W), row(V_W), row(NA_W), memblk, memblk,
                  _const_spec((1, d)), _const_spec(w2.shape), _const_spec((1, V_W)),
                  _const_spec((1, MEM_W)), _const_spec((MEM_W, MEM_W)),
                  _const_spec(wpm.shape), _const_spec(wpn.shape), _const_spec(wpx.shape),
                  _const_spec(wout.shape)],
        out_specs=row(d),
        out_shape=jax.ShapeDtypeStruct((n, d), F32),
        compiler_params=pltpu.CompilerParams(dimension_semantics=("parallel",),
                                             vmem_limit_bytes=VMEM_LIMIT),
        name="merge",
    )(x2, hf, hb, hna, k_mem, v_mem, g_mix, w2, g_mlstm, gmq,
      _group_mean_matrix(MEM_W, MEM_HEAD_DIM), wpm, wpn, wpx, wout)


def _ffn_kernel(x_ref, g_ref, wu_ref, wd_ref, o_ref, *, n_chunks):
    x = x_ref[...]
    h = _rms_rows(x, g_ref[...]).astype(BF16)
    ck = wu_ref.shape[1] // n_chunks
    acc = x
    for j in range(n_chunks):
        u = jnp.maximum(_dot(h, wu_ref[:, j * ck:(j + 1) * ck]), 0.0)
        acc = acc + _dot((u * u).astype(BF16), wd_ref[j * ck:(j + 1) * ck, :])
    o_ref[...] = acc


def _ffn(x2, g_ffn, w_up, w_down, tm):
    n, d = x2.shape
    row = pl.BlockSpec((tm, d), lambda i: (i, 0))
    return pl.pallas_call(
        functools.partial(_ffn_kernel, n_chunks=w_up.shape[1] // d),
        grid=(n // tm,),
        in_specs=[row, _const_spec((1, d)), _const_spec(w_up.shape), _const_spec(w_down.shape)],
        out_specs=row,
        out_shape=jax.ShapeDtypeStruct((n, d), F32),
        compiler_params=pltpu.CompilerParams(dimension_semantics=("parallel",),
                                             vmem_limit_bytes=VMEM_LIMIT),
        name="ffn",
    )(x2, g_ffn, w_up, w_down)


def _layer(x, mem, g_mix, w_in, conv_w, conv_b, b_igate, b_fgate, g_mlstm, w_proj_mlstm,
           g_na_q, g_na_k, rpb, w_proj_na, g_mem, w_mem_kv, g_mem_q, g_mem_k,
           w_proj_mem, w_out, g_ffn, w_up, w_down):
    bsz, s, d = x.shape
    n = bsz * s
    tm = min(512, s)
    x2 = x.reshape(n, d)

    o_qk, o_v = 0, 2 * QK_W
    o_o = o_v + V_W
    o_i = o_o + V_W
    o_f = o_i + 2 * MLSTM_HEADS
    o_na = o_f + 2 * MLSTM_HEADS
    o_qm = o_na + 3 * NA_W
    o_g = o_qm + MEM_W
    gate_pad = ((0, 0), (0, GATE_PAD - N_GATES // 2))
    w_b = w_in.astype(BF16)
    w_gi = jnp.pad(w_b[:, o_i:o_f], gate_pad)
    w_gf = jnp.pad(w_b[:, o_f:o_na], gate_pad)
    bias_i = jnp.pad(b_igate.reshape(1, -1).astype(F32), gate_pad)
    bias_f = jnp.pad(b_fgate.reshape(1, -1).astype(F32), gate_pad)
    w1 = jnp.concatenate([w_b[:, o_qk:o_o], w_gi, w_gf, w_b[:, o_na:o_qm]], axis=1)
    w2 = jnp.concatenate([w_b[:, o_o:o_i], w_b[:, o_qm:o_g], w_b[:, o_g:]], axis=1)
    row = lambda a: a.reshape(1, -1).astype(F32)

    qk, v, gi, gf, qn, kn, vn = _proj_in(x2, row(g_mix), w1, row(jnp.tile(g_na_q, NA_HEADS)),
                                         row(jnp.tile(g_na_k, NA_HEADS)), conv_w.astype(F32),
                                         row(conv_b), tm, s)
    k_mem, v_mem = _mem_kv(mem, row(g_mem), w_mem_kv.astype(BF16), row(jnp.tile(g_mem_k, MEM_HEADS)))

    L = min(MLSTM_CHUNK, s)
    hf, hb = _mlstm(qk.reshape(bsz, s, -1), v.reshape(bsz, s, -1), gi.reshape(bsz, s, -1),
                    gf.reshape(bsz, s, -1), bias_i, bias_f, L, 2 if bsz % 2 == 0 else 1)
    hna = _natten(qn.reshape(bsz, s, -1), kn.reshape(bsz, s, -1), vn.reshape(bsz, s, -1),
                  _na_bias_table(rpb), 4 if (s // GRID_W) % 4 == 0 else 1)

    x1 = _merge(x2, hf.reshape(n, -1), hb.reshape(n, -1), hna.reshape(n, -1), k_mem, v_mem,
                row(g_mix), w2, row(g_mlstm), row(jnp.tile(g_mem_q, MEM_HEADS)),
                w_proj_mlstm.astype(BF16), w_proj_na.astype(BF16), w_proj_mem.astype(BF16),
                w_out.astype(BF16), tm, s)
    out = _ffn(x1, row(g_ffn), w_up.astype(BF16), w_down.astype(BF16), tm)
    return out.reshape(bsz, s, d)


def kernel(x, mem, g_mix, w_in, conv_w, conv_b, b_igate, b_fgate, g_mlstm, w_proj_mlstm,
           g_na_q, g_na_k, rpb, w_proj_na, g_mem, w_mem_kv, g_mem_q, g_mem_k,
           w_proj_mem, w_out, g_ffn, w_up, w_down):
    for l in range(g_mix.shape[0]):
        x = _layer(x, mem, g_mix[l], w_in[l], conv_w[l], conv_b[l], b_igate[l], b_fgate[l],
                   g_mlstm[l], w_proj_mlstm[l], g_na_q[l], g_na_k[l], rpb[l], w_proj_na[l],
                   g_mem[l], w_mem_kv[l], g_mem_q[l], g_mem_k[l], w_proj_mem[l], w_out[l],
                   g_ffn[l], w_up[l], w_down[l])
    return x
```

```python
import functools

import numpy as np
import jax
import jax.numpy as jnp
from jax import lax
from jax.experimental import pallas as pl
from jax.experimental.pallas import tpu as pltpu

GRID_W = 64
MLSTM_HEADS = 4
MLSTM_QK_DIM = 64
MLSTM_V_DIM = 128
MLSTM_CONV = 5
NA_HEADS = 8
NA_HEAD_DIM = 32
NA_WIN_ROWS = 8
NA_WIN_COLS = 16
MEM_HEADS = 4
MEM_HEAD_DIM = 64
N_BRANCH = 3
EPS = 1e-6

QK_W = MLSTM_HEADS * MLSTM_QK_DIM
V_W = MLSTM_HEADS * MLSTM_V_DIM
NA_W = NA_HEADS * NA_HEAD_DIM
MEM_W = MEM_HEADS * MEM_HEAD_DIM
N_GATES = 4 * MLSTM_HEADS
LANES = 128
SUBLANES = 8
HALO = SUBLANES
GATE_PAD = LANES
NEG_BIG = -1e30
LOG2E = 1.4426950408889634

MLSTM_CHUNK = 256
VMEM_LIMIT = 48 * 1024 * 1024

BF16 = jnp.bfloat16
F32 = jnp.float32


def _dot(a, b):
    return jnp.dot(a, b, preferred_element_type=F32)


def _dot_nt(a, b):
    return lax.dot_general(a, b, (((1,), (1,)), ((), ())), preferred_element_type=F32)


def _dot_tn(a, b):
    return lax.dot_general(a, b, (((0,), (0,)), ((), ())), preferred_element_type=F32)


def _split_dot(a, m_bf16):
    hi = a.astype(BF16)
    lo = (a - hi.astype(F32)).astype(BF16)
    return _dot(hi, m_bf16) + _dot(lo, m_bf16)


def _rms_rows(x, g):
    ms = jnp.mean(x * x, axis=-1, keepdims=True)
    return x * lax.rsqrt(ms + EPS) * g


def _group_mean_matrix(width, group):
    idx = np.arange(width) // group
    return jnp.asarray((idx[:, None] == idx[None, :]).astype(np.float32) / group, dtype=BF16)


def _const_spec(shape):
    nd = len(shape)
    return pl.BlockSpec(shape, lambda *_: (0,) * nd, pipeline_mode=pl.Buffered(1))


def _proj_in_kernel(x_ref, xp_ref, xn_ref, g_ref, w_ref, gm_ref, gq_ref, gk_ref, cw_ref, cb_ref,
                    qk_ref, v_ref, gi_ref, gf_ref, qn_ref, kn_ref, vn_ref, *, tiles_per_seq):
    i = pl.program_id(0)
    tm = x_ref.shape[0]
    g = g_ref[...]
    h = _rms_rows(x_ref[...], g).astype(BF16)
    pos = i % tiles_per_seq
    hp = jnp.where(pos > 0, _rms_rows(xp_ref[...], g), 0.0).astype(BF16)
    hn = jnp.where(pos < tiles_per_seq - 1, _rms_rows(xn_ref[...], g), 0.0).astype(BF16)
    o = 0
    z = _dot(jnp.concatenate([hp, h, hn], axis=0), w_ref[:, o:o + 2 * QK_W]); o += 2 * QK_W
    cw = cw_ref[...]
    acc = jnp.zeros((tm, 2 * QK_W), F32) + cb_ref[...]
    for j in range(MLSTM_CONV):
        d = j - MLSTM_CONV // 2
        tap = z if d == 0 else pltpu.roll(z, (-d) % z.shape[0], 0)
        acc = acc + tap[HALO:HALO + tm, :] * cw[j:j + 1, :]
    act = acc * jax.nn.sigmoid(acc)
    lane = lax.broadcasted_iota(jnp.int32, act.shape, 1)
    qk_ref[...] = jnp.where(lane < QK_W, act * (MLSTM_QK_DIM ** -0.5), act).astype(BF16)
    v_ref[...] = _dot(h, w_ref[:, o:o + V_W]).astype(BF16); o += V_W
    gi_ref[...] = _dot(h, w_ref[:, o:o + GATE_PAD]); o += GATE_PAD
    gf_ref[...] = _dot(h, w_ref[:, o:o + GATE_PAD]); o += GATE_PAD
    q = _dot(h, w_ref[:, o:o + NA_W]); o += NA_W
    k = _dot(h, w_ref[:, o:o + NA_W]); o += NA_W
    vn_ref[...] = _dot(h, w_ref[:, o:o + NA_W]).astype(BF16)
    gm = gm_ref[...]
    qn = q * lax.rsqrt(_split_dot(q * q, gm) + EPS) * gq_ref[...]
    qn_ref[...] = (qn * (NA_HEAD_DIM ** -0.5 * LOG2E)).astype(BF16)
    kn_ref[...] = (k * lax.rsqrt(_split_dot(k * k, gm) + EPS) * gk_ref[...]).astype(BF16)


def _proj_in(x2, g_mix, w1, gq, gk, conv_w, conv_b, tm, s):
    n, d = x2.shape
    w_cols = w1.shape[1]
    hb = tm // HALO
    row = lambda width: pl.BlockSpec((tm, width), lambda i: (i, 0))
    prev = pl.BlockSpec((HALO, d), lambda i: (jnp.maximum(i * hb - 1, 0), 0))
    nxt = pl.BlockSpec((HALO, d), lambda i: (jnp.minimum((i + 1) * hb, n // HALO - 1), 0))
    return pl.pallas_call(
        functools.partial(_proj_in_kernel, tiles_per_seq=s // tm),
        grid=(n // tm,),
        in_specs=[row(d), prev, nxt, _const_spec((1, d)), _const_spec((d, w_cols)),
                  _const_spec((NA_W, NA_W)), _const_spec((1, NA_W)), _const_spec((1, NA_W)),
                  _const_spec((MLSTM_CONV, 2 * QK_W)), _const_spec((1, 2 * QK_W))],
        out_specs=[row(2 * QK_W), row(V_W), row(GATE_PAD), row(GATE_PAD), row(NA_W), row(NA_W), row(NA_W)],
        out_shape=[jax.ShapeDtypeStruct((n, 2 * QK_W), BF16),
                   jax.ShapeDtypeStruct((n, V_W), BF16),
                   jax.ShapeDtypeStruct((n, GATE_PAD), F32),
                   jax.ShapeDtypeStruct((n, GATE_PAD), F32),
                   jax.ShapeDtypeStruct((n, NA_W), BF16),
                   jax.ShapeDtypeStruct((n, NA_W), BF16),
                   jax.ShapeDtypeStruct((n, NA_W), BF16)],
        compiler_params=pltpu.CompilerParams(dimension_semantics=("parallel",),
                                             vmem_limit_bytes=VMEM_LIMIT),
        name="proj_in",
    )(x2, x2, x2, g_mix, w1, _group_mean_matrix(NA_W, NA_HEAD_DIM), gq, gk, conv_w, conv_b)


def _mem_kv_kernel(mem_ref, g_ref, w_ref, gm_ref, gk_ref, k_ref, v_ref):
    h = _rms_rows(mem_ref[...], g_ref[...]).astype(BF16)
    k = _dot(h, w_ref[:, :MEM_W])
    v_ref[...] = _dot(h, w_ref[:, MEM_W:]).astype(BF16)
    k_ref[...] = (k * lax.rsqrt(_split_dot(k * k, gm_ref[...]) + EPS) * gk_ref[...]).astype(BF16)


def _mem_kv(mem, g_mem, w_kv, gk):
    b, m, d = mem.shape
    blk = lambda width: pl.BlockSpec((None, m, width), lambda i: (i, 0, 0))
    return pl.pallas_call(
        _mem_kv_kernel,
        grid=(b,),
        in_specs=[blk(d), _const_spec((1, d)), _const_spec((d, 2 * MEM_W)),
                  _const_spec((MEM_W, MEM_W)), _const_spec((1, MEM_W))],
        out_specs=[blk(MEM_W), blk(MEM_W)],
        out_shape=[jax.ShapeDtypeStruct((b, m, MEM_W), BF16)] * 2,
        compiler_params=pltpu.CompilerParams(dimension_semantics=("parallel",),
                                             vmem_limit_bytes=VMEM_LIMIT),
        name="mem_kv",
    )(mem, g_mem, w_kv, _group_mean_matrix(MEM_W, MEM_HEAD_DIM), gk)


def _log_sigmoid(x):
    return jnp.minimum(x, 0.0) - jnp.log1p(jnp.exp(-jnp.abs(x)))


def _cummax_rows(x, reverse):
    n = x.shape[0]
    row = lax.broadcasted_iota(jnp.int32, x.shape, 0)
    sh = 1
    while sh < n:
        if reverse:
            shifted, ok = pltpu.roll(x, n - sh, 0), row < n - sh
        else:
            shifted, ok = pltpu.roll(x, sh, 0), row >= sh
        x = jnp.where(ok, jnp.maximum(x, shifted), x)
        sh *= 2
    return x


def _mlstm_gates(reverse, ig, fpre, m_st):
    L = ig.shape[0]
    last = 0 if reverse else L - 1
    r_i = lax.broadcasted_iota(jnp.int32, (L, L), 0)
    c_i = lax.broadcasted_iota(jnp.int32, (L, L), 1)
    causal = (c_i >= r_i) if reverse else (c_i <= r_i)
    lf = _log_sigmoid(fpre)
    l1 = lf.astype(BF16)
    e1 = lf - l1.astype(F32)
    l2 = e1.astype(BF16)
    l3 = (e1 - l2.astype(F32)).astype(BF16)
    bb = _dot(jnp.where(causal, 1.0, 0.0).astype(BF16), jnp.concatenate([l1, l2, l3], axis=1))
    b = bb[:, :LANES] + bb[:, LANES:2 * LANES] + bb[:, 2 * LANES:]
    a = ig - b
    big_m = jnp.maximum(m_st, _cummax_rows(a, reverse))
    a2 = a * LOG2E
    m2 = big_m * LOG2E
    return dict(
        causal=causal, last=last,
        wint=jnp.exp(m_st - big_m),
        eneg=jnp.exp(-(b + big_m)),
        m2=m2,
        a2_rows=a2.T,
        ws=jnp.exp2(a2 - m2[last:last + 1, :]),
        m_next=(b + big_m)[last:last + 1, :],
    )


def _mlstm_chains(d, g, qk, v, states):
    L = qk.shape[0]
    H = MLSTM_HEADS
    causal, last = g["causal"], g["last"]
    lane = lax.broadcasted_iota(jnp.int32, (L, LANES), 1)
    nums, new_states = [], []
    den_all = jnp.zeros((L, LANES), F32)
    for hd in range(H):
        c = d * H + hd
        pair = hd // 2
        in_head = (lane // MLSTM_QK_DIM) == (hd % 2)
        q_h = jnp.where(in_head, qk[:, pair * LANES:(pair + 1) * LANES], jnp.zeros((), BF16))
        k_p = qk[:, QK_W + pair * LANES:QK_W + (pair + 1) * LANES]
        v_ext = jnp.concatenate([v[:, hd * LANES:(hd + 1) * LANES],
                                 jnp.where(lane == c, 1.0, 0.0).astype(BF16)], axis=1)
        st = states[hd]
        decay_log = g["a2_rows"][c:c + 1, :] - g["m2"][:, c:c + 1]
        p = jnp.exp2(jnp.where(causal, decay_log, -jnp.inf))
        s_mat = _dot_nt(q_h, k_p) * p
        q_inter = q_h.astype(F32) * g["wint"][:, c:c + 1]
        lhs = jnp.concatenate([s_mat.astype(BF16), q_inter.astype(BF16)], axis=1)
        rhs = jnp.concatenate([v_ext, st.astype(BF16)], axis=0)
        num_ext = _dot(lhs, rhs)
        nums.append(num_ext[:, :LANES])
        den_all = den_all + num_ext[:, LANES:]
        kw = (k_p.astype(F32) * g["ws"][:, c:c + 1]).astype(BF16)
        new_states.append(g["wint"][last:last + 1, c:c + 1] * st + _dot_tn(kw, v_ext))
    r_all = 1.0 / jnp.maximum(jnp.abs(den_all), g["eneg"])
    h = jnp.concatenate([nums[hd] * r_all[:, d * H + hd:d * H + hd + 1] for hd in range(H)], axis=1)
    return h, new_states


def _mlstm_kernel(qkf_ref, qkb_ref, vf_ref, vb_ref, gif_ref, gib_ref, gff_ref, gfb_ref, bi_ref, bf_ref,
                  hf_ref, hb_ref, st_ref, m_ref):
    H = MLSTM_HEADS

    @pl.when(pl.program_id(1) == 0)
    def _():
        st_ref[...] = jnp.zeros_like(st_ref)
        m_ref[...] = jnp.full(m_ref.shape, -jnp.inf, F32)

    bi, bf = bi_ref[...], bf_ref[...]
    results = []
    for e in range(qkf_ref.shape[0]):
        g_f = _mlstm_gates(False, gif_ref[e] + bi, gff_ref[e] + bf, m_ref[2 * e, 0:1, :])
        g_b = _mlstm_gates(True, gib_ref[e] + bi, gfb_ref[e] + bf, m_ref[2 * e + 1, 0:1, :])
        base = 2 * H * e
        h_f, st_f = _mlstm_chains(0, g_f, qkf_ref[e], vf_ref[e], [st_ref[base + c] for c in range(H)])
        h_b, st_b = _mlstm_chains(1, g_b, qkb_ref[e], vb_ref[e], [st_ref[base + H + c] for c in range(H)])
        results.append((h_f, h_b, st_f + st_b, g_f["m_next"], g_b["m_next"]))
    for e, (h_f, h_b, sts, m_f, m_b) in enumerate(results):
        hf_ref[e] = h_f
        hb_ref[e] = h_b
        for c, st in enumerate(sts):
            st_ref[2 * H * e + c] = st
        m_ref[2 * e] = jnp.broadcast_to(m_f, m_ref.shape[1:])
        m_ref[2 * e + 1] = jnp.broadcast_to(m_b, m_ref.shape[1:])


def _mlstm(qk3, v3, gi3, gf3, bias_i, bias_f, L, nb):
    b, s, _ = qk3.shape
    nc = s // L
    fwd = lambda bi, c: (bi, c, 0)
    bwd = lambda bi, c: (bi, nc - 1 - c, 0)
    blk = lambda width, pos: pl.BlockSpec((nb, L, width), pos)
    return pl.pallas_call(
        _mlstm_kernel,
        grid=(b // nb, nc),
        in_specs=[blk(2 * QK_W, fwd), blk(2 * QK_W, bwd), blk(V_W, fwd), blk(V_W, bwd),
                  blk(GATE_PAD, fwd), blk(GATE_PAD, bwd), blk(GATE_PAD, fwd), blk(GATE_PAD, bwd),
                  _const_spec((1, GATE_PAD)), _const_spec((1, GATE_PAD))],
        out_specs=[blk(V_W, fwd), blk(V_W, bwd)],
        out_shape=[jax.ShapeDtypeStruct((b, s, V_W), F32)] * 2,
        scratch_shapes=[pltpu.VMEM((nb * 2 * MLSTM_HEADS, LANES, 2 * LANES), F32),
                        pltpu.VMEM((nb * 2, SUBLANES, LANES), F32)],
        compiler_params=pltpu.CompilerParams(dimension_semantics=("parallel", "arbitrary"),
                                             vmem_limit_bytes=VMEM_LIMIT),
        name="mlstm",
    )(qk3, qk3, v3, v3, gi3, gi3, gf3, gf3, bias_i, bias_f)


def _stack_heads(q, n_heads, head_dim):
    lane = lax.broadcasted_iota(jnp.int32, q.shape, 1)
    zero = jnp.zeros_like(q)
    return jnp.concatenate([jnp.where(lane // head_dim == h, q, zero) for h in range(n_heads)], axis=0)


def _unstack_heads(o, n_heads, head_dim):
    t = o.shape[0] // n_heads
    lane = lax.broadcasted_iota(jnp.int32, (t, o.shape[1]), 1)
    acc = jnp.zeros((t, o.shape[1]), o.dtype)
    for h in range(n_heads):
        acc = jnp.where(lane // head_dim == h, o[h * t:(h + 1) * t, :], acc)
    return acc


def _masked_attention(q_stacked, k, v, bias):
    sc = _dot_nt(q_stacked, k)
    if bias is not None:
        sc = sc + bias
    p = jnp.exp2(sc - jnp.max(sc, axis=-1, keepdims=True))
    o = _dot(p.astype(BF16), v)
    return o / jnp.sum(p, axis=-1, keepdims=True)


def _natten_kernel(q_ref, k_ref, v_ref, bias_ref, o_ref, *, rows, rows_per_step):
    for j in range(rows_per_step):
        r = pl.program_id(1) * rows_per_step + j
        rs = jnp.clip(r - NA_WIN_ROWS // 2, 0, rows - NA_WIN_ROWS)
        start = pl.multiple_of(rs * GRID_W, GRID_W)
        k_win = k_ref[pl.ds(start, NA_WIN_ROWS * GRID_W), :]
        v_win = v_ref[pl.ds(start, NA_WIN_ROWS * GRID_W), :]
        q = q_ref[j * GRID_W:(j + 1) * GRID_W, :]
        off = NA_WIN_ROWS - 1 - (r - rs)
        lane0 = pl.multiple_of((off // 2) * LANES, LANES)
        bias = bias_ref[off % 2, :, pl.ds(lane0, NA_WIN_ROWS * GRID_W)]
        o = _masked_attention(_stack_heads(q, NA_HEADS, NA_HEAD_DIM), k_win, v_win, bias)
        o_ref[j * GRID_W:(j + 1) * GRID_W, :] = _unstack_heads(o, NA_HEADS, NA_HEAD_DIM).astype(o_ref.dtype)


def _na_bias_kernel(rpb_ref, onehot_ref, mask_ref, o_ref):
    r = rpb_ref[...]
    r1 = r.astype(BF16)
    e1 = r - r1.astype(F32)
    r2 = e1.astype(BF16)
    r3 = (e1 - r2.astype(F32)).astype(BF16)
    oh = onehot_ref[...]
    o_ref[...] = (_dot(r1, oh) + _dot(r2, oh) + _dot(r3, oh)) * LOG2E + mask_ref[...]


def _na_bias_table(rpb):
    n_ro, n_co = 2 * NA_WIN_ROWS - 1, 2 * NA_WIN_COLS - 1
    c = np.arange(GRID_W)[:, None]
    kc = np.arange(GRID_W)[None, :]
    cs = np.clip(c - NA_WIN_COLS // 2, 0, GRID_W - NA_WIN_COLS)
    valid = (kc >= cs) & (kc < cs + NA_WIN_COLS)
    col_off = kc - c + NA_WIN_COLS - 1
    onehot = (np.arange(LANES)[:, None, None] == col_off[None]) & valid[None]
    onehot = jnp.asarray(onehot.reshape(LANES, GRID_W * GRID_W), dtype=BF16)
    mask = jnp.asarray(np.where(valid, 0.0, NEG_BIG).reshape(1, GRID_W * GRID_W), dtype=F32)
    rp = jnp.pad(rpb.astype(F32).reshape(NA_HEADS * n_ro, n_co), ((0, 0), (0, LANES - n_co)))
    toep = pl.pallas_call(
        _na_bias_kernel,
        out_shape=jax.ShapeDtypeStruct((NA_HEADS * n_ro, GRID_W * GRID_W), F32),
        name="na_bias",
    )(rp, onehot, mask)
    toep = toep.reshape(NA_HEADS, n_ro, GRID_W, GRID_W).transpose(0, 2, 1, 3)
    toep = toep.reshape(NA_HEADS * GRID_W, n_ro * GRID_W)
    width = (n_ro - 1) * GRID_W
    return jnp.stack([toep[:, :width], toep[:, GRID_W:]])


def _natten(qn, kn, vn, bias_tab, rows_per_step):
    b, s, _ = qn.shape
    rows = s // GRID_W
    full = pl.BlockSpec((None, s, NA_W), lambda bi, r: (bi, 0, 0))
    tile = pl.BlockSpec((None, rows_per_step * GRID_W, NA_W), lambda bi, r: (bi, r, 0))
    return pl.pallas_call(
        functools.partial(_natten_kernel, rows=rows, rows_per_step=rows_per_step),
        grid=(b, rows // rows_per_step),
        in_specs=[tile, full, full, _const_spec(bias_tab.shape)],
        out_specs=tile,
        out_shape=jax.ShapeDtypeStruct((b, s, NA_W), BF16),
        compiler_params=pltpu.CompilerParams(dimension_semantics=("parallel", "arbitrary"),
                                             vmem_limit_bytes=VMEM_LIMIT),
        name="natten",
    )(qn, kn, vn, bias_tab)


def _merge_kernel(x_ref, hf_ref, hb_ref, hna_ref, km_ref, vm_ref,
                  g_ref, w2_ref, gml_ref, gmq_ref, gm64_ref,
                  wpm_ref, wpn_ref, wpx_ref, wout_ref, o_ref):
    x = x_ref[...]
    d = x.shape[1]
    h = _rms_rows(x, g_ref[...]).astype(BF16)
    o_pre = _dot(h, w2_ref[:, :V_W])
    q_mem = _dot(h, w2_ref[:, V_W:V_W + MEM_W])
    g0 = V_W + MEM_W

    hm = hf_ref[...] + hb_ref[...]
    gml = gml_ref[...]
    parts = []
    for hd in range(MLSTM_HEADS):
        sl = slice(hd * MLSTM_V_DIM, (hd + 1) * MLSTM_V_DIM)
        parts.append(_rms_rows(hm[:, sl], gml[:, sl]))
    hm = (jnp.concatenate(parts, axis=1) * jax.nn.sigmoid(o_pre)).astype(BF16)
    y = jax.nn.sigmoid(_dot(h, w2_ref[:, g0:g0 + d])) * _dot(hm, wpm_ref[...])

    y = y + jax.nn.sigmoid(_dot(h, w2_ref[:, g0 + d:g0 + 2 * d])) * _dot(hna_ref[...], wpn_ref[...])

    qn = q_mem * lax.rsqrt(_split_dot(q_mem * q_mem, gm64_ref[...]) + EPS) * gmq_ref[...]
    qn = (qn * (MEM_HEAD_DIM ** -0.5 * LOG2E)).astype(BF16)
    att = _masked_attention(_stack_heads(qn, MEM_HEADS, MEM_HEAD_DIM), km_ref[...], vm_ref[...], None)
    h_mem = _unstack_heads(att, MEM_HEADS, MEM_HEAD_DIM).astype(BF16)
    y = y + jax.nn.sigmoid(_dot(h, w2_ref[:, g0 + 2 * d:g0 + 3 * d])) * _dot(h_mem, wpx_ref[...])

    o_ref[...] = x + _dot(y.astype(BF16), wout_ref[...])


def _merge(x2, hf, hb, hna, k_mem, v_mem, g_mix, w2, g_mlstm, gmq, wpm, wpn, wpx, wout, tm, s):
    n, d = x2.shape
    m = k_mem.shape[1]
    per_b = s // tm
    row = lambda width: pl.BlockSpec((tm, width), lambda i: (i, 0))
    memblk = pl.BlockSpec((None, m, MEM_W), lambda i: (i // per_b, 0, 0))
    return pl.pallas_call(
        _merge_kernel,
        grid=(n // tm,),
        in_specs=[row(d), row(V_W), row(V_W), row(NA_W), memblk, memblk,
                  _const_spec((1, d)), _const_spec(w2.shape), _const_spec((1, V_W)),
                  _const_spec((1, MEM_W)), _const_spec((MEM_W, MEM_W)),
                  _const_spec(wpm.shape), _const_spec(wpn.shape), _const_spec(wpx.shape),
                  _const_spec(wout.shape)],
        out_specs=row(d),
        out_shape=jax.ShapeDtypeStruct((n, d), F32),
        compiler_params=pltpu.CompilerParams(dimension_semantics=("parallel",),
                                             vmem_limit_bytes=VMEM_LIMIT),
        name="merge",
    )(x2, hf, hb, hna, k_mem, v_mem, g_mix, w2, g_mlstm, gmq,
      _group_mean_matrix(MEM_W, MEM_HEAD_DIM), wpm, wpn, wpx, wout)


def _ffn_kernel(x_ref, g_ref, wu_ref, wd_ref, o_ref, *, n_chunks):
    x = x_ref[...]
    h = _rms_rows(x, g_ref[...]).astype(BF16)
    ck = wu_ref.shape[1] // n_chunks
    acc = x
    for j in range(n_chunks):
        u = jnp.maximum(_dot(h, wu_ref[:, j * ck:(j + 1) * ck]), 0.0)
        acc = acc + _dot((u * u).astype(BF16), wd_ref[j * ck:(j + 1) * ck, :])
    o_ref[...] = acc


def _ffn(x2, g_ffn, w_up, w_down, tm):
    n, d = x2.shape
    row = pl.BlockSpec((tm, d), lambda i: (i, 0))
    return pl.pallas_call(
        functools.partial(_ffn_kernel, n_chunks=w_up.shape[1] // d),
        grid=(n // tm,),
        in_specs=[row, _const_spec((1, d)), _const_spec(w_up.shape), _const_spec(w_down.shape)],
        out_specs=row,
        out_shape=jax.ShapeDtypeStruct((n, d), F32),
        compiler_params=pltpu.CompilerParams(dimension_semantics=("parallel",),
                                             vmem_limit_bytes=VMEM_LIMIT),
        name="ffn",
    )(x2, g_ffn, w_up, w_down)


def _layer(x, mem, g_mix, w_in, conv_w, conv_b, b_igate, b_fgate, g_mlstm, w_proj_mlstm,
           g_na_q, g_na_k, rpb, w_proj_na, g_mem, w_mem_kv, g_mem_q, g_mem_k,
           w_proj_mem, w_out, g_ffn, w_up, w_down):
    bsz, s, d = x.shape
    n = bsz * s
    tm = min(512, s)
    x2 = x.reshape(n, d)

    o_qk, o_v = 0, 2 * QK_W
    o_o = o_v + V_W
    o_i = o_o + V_W
    o_f = o_i + 2 * MLSTM_HEADS
    o_na = o_f + 2 * MLSTM_HEADS
    o_qm = o_na + 3 * NA_W
    o_g = o_qm + MEM_W
    gate_pad = ((0, 0), (0, GATE_PAD - N_GATES // 2))
    w_b = w_in.astype(BF16)
    w_gi = jnp.pad(w_b[:, o_i:o_f], gate_pad)
    w_gf = jnp.pad(w_b[:, o_f:o_na], gate_pad)
    bias_i = jnp.pad(b_igate.reshape(1, -1).astype(F32), gate_pad)
    bias_f = jnp.pad(b_fgate.reshape(1, -1).astype(F32), gate_pad)
    w1 = jnp.concatenate([w_b[:, o_qk:o_o], w_gi, w_gf, w_b[:, o_na:o_qm]], axis=1)
    w2 = jnp.concatenate([w_b[:, o_o:o_i], w_b[:, o_qm:o_g], w_b[:, o_g:]], axis=1)
    row = lambda a: a.reshape(1, -1).astype(F32)

    qk, v, gi, gf, qn, kn, vn = _proj_in(x2, row(g_mix), w1, row(jnp.tile(g_na_q, NA_HEADS)),
                                         row(jnp.tile(g_na_k, NA_HEADS)), conv_w.astype(F32),
                                         row(conv_b), min(1024, s), s)
    k_mem, v_mem = _mem_kv(mem, row(g_mem), w_mem_kv.astype(BF16), row(jnp.tile(g_mem_k, MEM_HEADS)))

    L = min(MLSTM_CHUNK, s)
    hf, hb = _mlstm(qk.reshape(bsz, s, -1), v.reshape(bsz, s, -1), gi.reshape(bsz, s, -1),
                    gf.reshape(bsz, s, -1), bias_i, bias_f, L, 4 if bsz % 4 == 0 else 1)
    hna = _natten(qn.reshape(bsz, s, -1), kn.reshape(bsz, s, -1), vn.reshape(bsz, s, -1),
                  _na_bias_table(rpb), 8 if (s // GRID_W) % 8 == 0 else 1)

    x1 = _merge(x2, hf.reshape(n, -1), hb.reshape(n, -1), hna.reshape(n, -1), k_mem, v_mem,
                row(g_mix), w2, row(g_mlstm), row(jnp.tile(g_mem_q, MEM_HEADS)),
                w_proj_mlstm.astype(BF16), w_proj_na.astype(BF16), w_proj_mem.astype(BF16),
                w_out.astype(BF16), tm, s)
    out = _ffn(x1, row(g_ffn), w_up.astype(BF16), w_down.astype(BF16), tm)
    return out.reshape(bsz, s, d)


def kernel(x, mem, g_mix, w_in, conv_w, conv_b, b_igate, b_fgate, g_mlstm, w_proj_mlstm,
           g_na_q, g_na_k, rpb, w_proj_na, g_mem, w_mem_kv, g_mem_q, g_mem_k,
           w_proj_mem, w_out, g_ffn, w_up, w_down):
    for l in range(g_mix.shape[0]):
        x = _layer(x, mem, g_mix[l], w_in[l], conv_w[l], conv_b[l], b_igate[l], b_fgate[l],
                   g_mlstm[l], w_proj_mlstm[l], g_na_q[l], g_na_k[l], rpb[l], w_proj_na[l],
                   g_mem[l], w_mem_kv[l], g_mem_q[l], g_mem_k[l], w_proj_mem[l], w_out[l],
                   g_ffn[l], w_up[l], w_down[l])
    return x
```

```python
import functools

import numpy as np
import jax
import jax.numpy as jnp
from jax import lax
from jax.experimental import pallas as pl
from jax.experimental.pallas import tpu as pltpu

GRID_W = 64
MLSTM_HEADS = 4
MLSTM_QK_DIM = 64
MLSTM_V_DIM = 128
MLSTM_CONV = 5
NA_HEADS = 8
NA_HEAD_DIM = 32
NA_WIN_ROWS = 8
NA_WIN_COLS = 16
MEM_HEADS = 4
MEM_HEAD_DIM = 64
N_BRANCH = 3
EPS = 1e-6

QK_W = MLSTM_HEADS * MLSTM_QK_DIM
V_W = MLSTM_HEADS * MLSTM_V_DIM
NA_W = NA_HEADS * NA_HEAD_DIM
MEM_W = MEM_HEADS * MEM_HEAD_DIM
N_GATES = 4 * MLSTM_HEADS
LANES = 128
SUBLANES = 8
HALO = SUBLANES
GATE_PAD = LANES
NEG_BIG = -1e30
LOG2E = 1.4426950408889634

ROW_TILE = 1024
MLSTM_CHUNK = 512
VMEM_LIMIT = 48 * 1024 * 1024

BF16 = jnp.bfloat16
F32 = jnp.float32


def _dot(a, b):
    return jnp.dot(a, b, preferred_element_type=F32)


def _dot_nt(a, b):
    return lax.dot_general(a, b, (((1,), (1,)), ((), ())), preferred_element_type=F32)


def _dot_tn(a, b):
    return lax.dot_general(a, b, (((0,), (0,)), ((), ())), preferred_element_type=F32)


def _split_dot(a, m_bf16):
    hi = a.astype(BF16)
    lo = (a - hi.astype(F32)).astype(BF16)
    return _dot(hi, m_bf16) + _dot(lo, m_bf16)


def _rms_rows(x, g):
    ms = jnp.mean(x * x, axis=-1, keepdims=True)
    return x * lax.rsqrt(ms + EPS) * g


def _group_mean_matrix(width, group):
    idx = np.arange(width) // group
    return jnp.asarray((idx[:, None] == idx[None, :]).astype(np.float32) / group, dtype=BF16)


def _const_spec(shape):
    nd = len(shape)
    return pl.BlockSpec(shape, lambda *_: (0,) * nd, pipeline_mode=pl.Buffered(1))


def _proj_in_kernel(x_ref, xp_ref, xn_ref, g_ref, w_ref, gm_ref, gq_ref, gk_ref, cw_ref, cb_ref,
                    qk_ref, v_ref, gi_ref, gf_ref, qn_ref, kn_ref, vn_ref, *, tiles_per_seq):
    i = pl.program_id(0)
    tm = x_ref.shape[0]
    g = g_ref[...]
    h = _rms_rows(x_ref[...], g).astype(BF16)
    pos = i % tiles_per_seq
    hp = jnp.where(pos > 0, _rms_rows(xp_ref[...], g), 0.0).astype(BF16)
    hn = jnp.where(pos < tiles_per_seq - 1, _rms_rows(xn_ref[...], g), 0.0).astype(BF16)
    o = 0
    z = _dot(jnp.concatenate([hp, h, hn], axis=0), w_ref[:, o:o + 2 * QK_W]); o += 2 * QK_W
    cw = cw_ref[...]
    acc = jnp.zeros((tm, 2 * QK_W), F32) + cb_ref[...]
    for j in range(MLSTM_CONV):
        d = j - MLSTM_CONV // 2
        tap = z if d == 0 else pltpu.roll(z, (-d) % z.shape[0], 0)
        acc = acc + tap[HALO:HALO + tm, :] * cw[j:j + 1, :]
    act = acc * jax.nn.sigmoid(acc)
    lane = lax.broadcasted_iota(jnp.int32, act.shape, 1)
    qk_ref[...] = jnp.where(lane < QK_W, act * (MLSTM_QK_DIM ** -0.5), act).astype(BF16)
    v_ref[...] = _dot(h, w_ref[:, o:o + V_W]).astype(BF16); o += V_W
    gi_ref[...] = _dot(h, w_ref[:, o:o + GATE_PAD]); o += GATE_PAD
    gf_ref[...] = _dot(h, w_ref[:, o:o + GATE_PAD]); o += GATE_PAD
    q = _dot(h, w_ref[:, o:o + NA_W]); o += NA_W
    k = _dot(h, w_ref[:, o:o + NA_W]); o += NA_W
    vn_ref[...] = _dot(h, w_ref[:, o:o + NA_W]).astype(BF16)
    gm = gm_ref[...]
    qn = q * lax.rsqrt(_split_dot(q * q, gm) + EPS) * gq_ref[...]
    qn_ref[...] = (qn * (NA_HEAD_DIM ** -0.5 * LOG2E)).astype(BF16)
    kn_ref[...] = (k * lax.rsqrt(_split_dot(k * k, gm) + EPS) * gk_ref[...]).astype(BF16)


def _proj_in(x2, g_mix, w1, gq, gk, conv_w, conv_b, tm, s):
    n, d = x2.shape
    w_cols = w1.shape[1]
    hb = tm // HALO
    row = lambda width: pl.BlockSpec((tm, width), lambda i: (i, 0))
    prev = pl.BlockSpec((HALO, d), lambda i: (jnp.maximum(i * hb - 1, 0), 0))
    nxt = pl.BlockSpec((HALO, d), lambda i: (jnp.minimum((i + 1) * hb, n // HALO - 1), 0))
    return pl.pallas_call(
        functools.partial(_proj_in_kernel, tiles_per_seq=s // tm),
        grid=(n // tm,),
        in_specs=[row(d), prev, nxt, _const_spec((1, d)), _const_spec((d, w_cols)),
                  _const_spec((NA_W, NA_W)), _const_spec((1, NA_W)), _const_spec((1, NA_W)),
                  _const_spec((MLSTM_CONV, 2 * QK_W)), _const_spec((1, 2 * QK_W))],
        out_specs=[row(2 * QK_W), row(V_W), row(GATE_PAD), row(GATE_PAD), row(NA_W), row(NA_W), row(NA_W)],
        out_shape=[jax.ShapeDtypeStruct((n, 2 * QK_W), BF16),
                   jax.ShapeDtypeStruct((n, V_W), BF16),
                   jax.ShapeDtypeStruct((n, GATE_PAD), F32),
                   jax.ShapeDtypeStruct((n, GATE_PAD), F32),
                   jax.ShapeDtypeStruct((n, NA_W), BF16),
                   jax.ShapeDtypeStruct((n, NA_W), BF16),
                   jax.ShapeDtypeStruct((n, NA_W), BF16)],
        compiler_params=pltpu.CompilerParams(dimension_semantics=("parallel",),
                                             vmem_limit_bytes=VMEM_LIMIT),
        name="proj_in",
    )(x2, x2, x2, g_mix, w1, _group_mean_matrix(NA_W, NA_HEAD_DIM), gq, gk, conv_w, conv_b)


def _mem_kv_kernel(mem_ref, g_ref, w_ref, gm_ref, gk_ref, k_ref, v_ref):
    h = _rms_rows(mem_ref[...], g_ref[...]).astype(BF16)
    k = _dot(h, w_ref[:, :MEM_W])
    v_ref[...] = _dot(h, w_ref[:, MEM_W:]).astype(BF16)
    k_ref[...] = (k * lax.rsqrt(_split_dot(k * k, gm_ref[...]) + EPS) * gk_ref[...]).astype(BF16)


def _mem_kv(mem, g_mem, w_kv, gk):
    b, m, d = mem.shape
    blk = lambda width: pl.BlockSpec((None, m, width), lambda i: (i, 0, 0))
    return pl.pallas_call(
        _mem_kv_kernel,
        grid=(b,),
        in_specs=[blk(d), _const_spec((1, d)), _const_spec((d, 2 * MEM_W)),
                  _const_spec((MEM_W, MEM_W)), _const_spec((1, MEM_W))],
        out_specs=[blk(MEM_W), blk(MEM_W)],
        out_shape=[jax.ShapeDtypeStruct((b, m, MEM_W), BF16)] * 2,
        compiler_params=pltpu.CompilerParams(dimension_semantics=("parallel",),
                                             vmem_limit_bytes=VMEM_LIMIT),
        name="mem_kv",
    )(mem, g_mem, w_kv, _group_mean_matrix(MEM_W, MEM_HEAD_DIM), gk)


def _log_sigmoid(x):
    return jnp.minimum(x, 0.0) - jnp.log1p(jnp.exp(-jnp.abs(x)))


def _cummax_rows(x, reverse):
    n = x.shape[0]
    row = lax.broadcasted_iota(jnp.int32, x.shape, 0)
    sh = 1
    while sh < n:
        if reverse:
            shifted, ok = pltpu.roll(x, n - sh, 0), row < n - sh
        else:
            shifted, ok = pltpu.roll(x, sh, 0), row >= sh
        x = jnp.where(ok, jnp.maximum(x, shifted), x)
        sh *= 2
    return x


def _mlstm_gates(reverse, ig, fpre, m_st):
    L = ig.shape[0]
    last = 0 if reverse else L - 1
    r_i = lax.broadcasted_iota(jnp.int32, (L, L), 0)
    c_i = lax.broadcasted_iota(jnp.int32, (L, L), 1)
    causal = (c_i >= r_i) if reverse else (c_i <= r_i)
    lf = _log_sigmoid(fpre)
    l1 = lf.astype(BF16)
    e1 = lf - l1.astype(F32)
    l2 = e1.astype(BF16)
    l3 = (e1 - l2.astype(F32)).astype(BF16)
    bb = _dot(jnp.where(causal, 1.0, 0.0).astype(BF16), jnp.concatenate([l1, l2, l3], axis=1))
    b = bb[:, :LANES] + bb[:, LANES:2 * LANES] + bb[:, 2 * LANES:]
    a = ig - b
    big_m = jnp.maximum(m_st, _cummax_rows(a, reverse))
    a2 = a * LOG2E
    m2 = big_m * LOG2E
    return dict(
        causal=causal, last=last,
        wint=jnp.exp(m_st - big_m),
        eneg=jnp.exp(-(b + big_m)),
        m2=m2,
        a2_rows=a2.T,
        ws=jnp.exp2(a2 - m2[last:last + 1, :]),
        m_next=(b + big_m)[last:last + 1, :],
    )


def _mlstm_chains(d, g, qk, v, states):
    L = qk.shape[0]
    H = MLSTM_HEADS
    causal, last = g["causal"], g["last"]
    lane = lax.broadcasted_iota(jnp.int32, (L, LANES), 1)
    nums, new_states = [], []
    den_all = jnp.zeros((L, LANES), F32)
    for hd in range(H):
        c = d * H + hd
        pair = hd // 2
        in_head = (lane // MLSTM_QK_DIM) == (hd % 2)
        q_h = jnp.where(in_head, qk[:, pair * LANES:(pair + 1) * LANES], jnp.zeros((), BF16))
        k_p = qk[:, QK_W + pair * LANES:QK_W + (pair + 1) * LANES]
        v_ext = jnp.concatenate([v[:, hd * LANES:(hd + 1) * LANES],
                                 jnp.where(lane == c, 1.0, 0.0).astype(BF16)], axis=1)
        st = states[hd]
        decay_log = g["a2_rows"][c:c + 1, :] - g["m2"][:, c:c + 1]
        p = jnp.exp2(jnp.where(causal, decay_log, -jnp.inf))
        s_mat = _dot_nt(q_h, k_p) * p
        q_inter = q_h.astype(F32) * g["wint"][:, c:c + 1]
        lhs = jnp.concatenate([s_mat.astype(BF16), q_inter.astype(BF16)], axis=1)
        rhs = jnp.concatenate([v_ext, st.astype(BF16)], axis=0)
        num_ext = _dot(lhs, rhs)
        nums.append(num_ext[:, :LANES])
        den_all = den_all + num_ext[:, LANES:]
        kw = (k_p.astype(F32) * g["ws"][:, c:c + 1]).astype(BF16)
        new_states.append(g["wint"][last:last + 1, c:c + 1] * st + _dot_tn(kw, v_ext))
    r_all = 1.0 / jnp.maximum(jnp.abs(den_all), g["eneg"])
    h = jnp.concatenate([nums[hd] * r_all[:, d * H + hd:d * H + hd + 1] for hd in range(H)], axis=1)
    return h, new_states


def _mlstm_kernel(qkf_ref, qkb_ref, vf_ref, vb_ref, gif_ref, gib_ref, gff_ref, gfb_ref, bi_ref, bf_ref,
                  hf_ref, hb_ref, st_ref, m_ref):
    H = MLSTM_HEADS

    @pl.when(pl.program_id(1) == 0)
    def _():
        st_ref[...] = jnp.zeros_like(st_ref)
        m_ref[...] = jnp.full(m_ref.shape, -jnp.inf, F32)

    bi, bf = bi_ref[...], bf_ref[...]
    results = []
    for e in range(qkf_ref.shape[0]):
        g_f = _mlstm_gates(False, gif_ref[e] + bi, gff_ref[e] + bf, m_ref[2 * e, 0:1, :])
        g_b = _mlstm_gates(True, gib_ref[e] + bi, gfb_ref[e] + bf, m_ref[2 * e + 1, 0:1, :])
        base = 2 * H * e
        h_f, st_f = _mlstm_chains(0, g_f, qkf_ref[e], vf_ref[e], [st_ref[base + c] for c in range(H)])
        h_b, st_b = _mlstm_chains(1, g_b, qkb_ref[e], vb_ref[e], [st_ref[base + H + c] for c in range(H)])
        results.append((h_f, h_b, st_f + st_b, g_f["m_next"], g_b["m_next"]))
    for e, (h_f, h_b, sts, m_f, m_b) in enumerate(results):
        hf_ref[e] = h_f
        hb_ref[e] = h_b
        for c, st in enumerate(sts):
            st_ref[2 * H * e + c] = st
        m_ref[2 * e] = jnp.broadcast_to(m_f, m_ref.shape[1:])
        m_ref[2 * e + 1] = jnp.broadcast_to(m_b, m_ref.shape[1:])


def _mlstm(qk3, v3, gi3, gf3, bias_i, bias_f, L, nb):
    b, s, _ = qk3.shape
    nc = s // L
    fwd = lambda bi, c: (bi, c, 0)
    bwd = lambda bi, c: (bi, nc - 1 - c, 0)
    blk = lambda width, pos: pl.BlockSpec((nb, L, width), pos)
    return pl.pallas_call(
        _mlstm_kernel,
        grid=(b // nb, nc),
        in_specs=[blk(2 * QK_W, fwd), blk(2 * QK_W, bwd), blk(V_W, fwd), blk(V_W, bwd),
                  blk(GATE_PAD, fwd), blk(GATE_PAD, bwd), blk(GATE_PAD, fwd), blk(GATE_PAD, bwd),
                  _const_spec((1, GATE_PAD)), _const_spec((1, GATE_PAD))],
        out_specs=[blk(V_W, fwd), blk(V_W, bwd)],
        out_shape=[jax.ShapeDtypeStruct((b, s, V_W), F32)] * 2,
        scratch_shapes=[pltpu.VMEM((nb * 2 * MLSTM_HEADS, LANES, 2 * LANES), F32),
                        pltpu.VMEM((nb * 2, SUBLANES, LANES), F32)],
        compiler_params=pltpu.CompilerParams(dimension_semantics=("parallel", "arbitrary"),
                                             vmem_limit_bytes=VMEM_LIMIT),
        name="mlstm",
    )(qk3, qk3, v3, v3, gi3, gi3, gf3, gf3, bias_i, bias_f)


def _stack_heads(q, n_heads, head_dim):
    lane = lax.broadcasted_iota(jnp.int32, q.shape, 1)
    zero = jnp.zeros_like(q)
    return jnp.concatenate([jnp.where(lane // head_dim == h, q, zero) for h in range(n_heads)], axis=0)


def _unstack_heads(o, n_heads, head_dim):
    t = o.shape[0] // n_heads
    lane = lax.broadcasted_iota(jnp.int32, (t, o.shape[1]), 1)
    acc = jnp.zeros((t, o.shape[1]), o.dtype)
    for h in range(n_heads):
        acc = jnp.where(lane // head_dim == h, o[h * t:(h + 1) * t, :], acc)
    return acc


def _masked_attention(q_stacked, k, v, bias):
    sc = _dot_nt(q_stacked, k)
    if bias is not None:
        sc = sc + bias
    p = jnp.exp2(sc - jnp.max(sc, axis=-1, keepdims=True))
    o = _dot(p.astype(BF16), v)
    return o / jnp.sum(p, axis=-1, keepdims=True)


def _natten_kernel(q_ref, k_ref, v_ref, bias_ref, o_ref, *, rows, rows_per_step):
    for j in range(rows_per_step):
        r = pl.program_id(1) * rows_per_step + j
        rs = jnp.clip(r - NA_WIN_ROWS // 2, 0, rows - NA_WIN_ROWS)
        start = pl.multiple_of(rs * GRID_W, GRID_W)
        k_win = k_ref[pl.ds(start, NA_WIN_ROWS * GRID_W), :]
        v_win = v_ref[pl.ds(start, NA_WIN_ROWS * GRID_W), :]
        q = q_ref[j * GRID_W:(j + 1) * GRID_W, :]
        off = NA_WIN_ROWS - 1 - (r - rs)
        lane0 = pl.multiple_of((off // 2) * LANES, LANES)
        bias = bias_ref[off % 2, :, pl.ds(lane0, NA_WIN_ROWS * GRID_W)]
        o = _masked_attention(_stack_heads(q, NA_HEADS, NA_HEAD_DIM), k_win, v_win, bias)
        o_ref[j * GRID_W:(j + 1) * GRID_W, :] = _unstack_heads(o, NA_HEADS, NA_HEAD_DIM).astype(o_ref.dtype)


def _na_bias_kernel(rpb_ref, onehot_ref, mask_ref, o_ref):
    r = rpb_ref[...]
    r1 = r.astype(BF16)
    e1 = r - r1.astype(F32)
    r2 = e1.astype(BF16)
    r3 = (e1 - r2.astype(F32)).astype(BF16)
    oh = onehot_ref[...]
    o_ref[...] = (_dot(r1, oh) + _dot(r2, oh) + _dot(r3, oh)) * LOG2E + mask_ref[...]


def _na_bias_table(rpb):
    n_ro, n_co = 2 * NA_WIN_ROWS - 1, 2 * NA_WIN_COLS - 1
    c = np.arange(GRID_W)[:, None]
    kc = np.arange(GRID_W)[None, :]
    cs = np.clip(c - NA_WIN_COLS // 2, 0, GRID_W - NA_WIN_COLS)
    valid = (kc >= cs) & (kc < cs + NA_WIN_COLS)
    col_off = kc - c + NA_WIN_COLS - 1
    onehot = (np.arange(LANES)[:, None, None] == col_off[None]) & valid[None]
    onehot = jnp.asarray(onehot.reshape(LANES, GRID_W * GRID_W), dtype=BF16)
    mask = jnp.asarray(np.where(valid, 0.0, NEG_BIG).reshape(1, GRID_W * GRID_W), dtype=F32)
    rp = jnp.pad(rpb.astype(F32).reshape(NA_HEADS * n_ro, n_co), ((0, 0), (0, LANES - n_co)))
    toep = pl.pallas_call(
        _na_bias_kernel,
        out_shape=jax.ShapeDtypeStruct((NA_HEADS * n_ro, GRID_W * GRID_W), F32),
        name="na_bias",
    )(rp, onehot, mask)
    toep = toep.reshape(NA_HEADS, n_ro, GRID_W, GRID_W).transpose(0, 2, 1, 3)
    toep = toep.reshape(NA_HEADS * GRID_W, n_ro * GRID_W)
    width = (n_ro - 1) * GRID_W
    return jnp.stack([toep[:, :width], toep[:, GRID_W:]])


def _natten(qn, kn, vn, bias_tab, rows_per_step):
    b, s, _ = qn.shape
    rows = s // GRID_W
    full = pl.BlockSpec((None, s, NA_W), lambda bi, r: (bi, 0, 0))
    tile = pl.BlockSpec((None, rows_per_step * GRID_W, NA_W), lambda bi, r: (bi, r, 0))
    return pl.pallas_call(
        functools.partial(_natten_kernel, rows=rows, rows_per_step=rows_per_step),
        grid=(b, rows // rows_per_step),
        in_specs=[tile, full, full, _const_spec(bias_tab.shape)],
        out_specs=tile,
        out_shape=jax.ShapeDtypeStruct((b, s, NA_W), BF16),
        compiler_params=pltpu.CompilerParams(dimension_semantics=("parallel", "arbitrary"),
                                             vmem_limit_bytes=VMEM_LIMIT),
        name="natten",
    )(qn, kn, vn, bias_tab)


def _merge_kernel(x_ref, hf_ref, hb_ref, hna_ref, km_ref, vm_ref,
                  g_ref, w2_ref, gml_ref, gmq_ref, gm64_ref,
                  wpm_ref, wpn_ref, wpx_ref, wout_ref, o_ref):
    x = x_ref[...]
    d = x.shape[1]
    h = _rms_rows(x, g_ref[...]).astype(BF16)
    o_pre = _dot(h, w2_ref[:, :V_W])
    q_mem = _dot(h, w2_ref[:, V_W:V_W + MEM_W])
    g0 = V_W + MEM_W

    hm = hf_ref[...] + hb_ref[...]
    gml = gml_ref[...]
    parts = []
    for hd in range(MLSTM_HEADS):
        sl = slice(hd * MLSTM_V_DIM, (hd + 1) * MLSTM_V_DIM)
        parts.append(_rms_rows(hm[:, sl], gml[:, sl]))
    hm = (jnp.concatenate(parts, axis=1) * jax.nn.sigmoid(o_pre)).astype(BF16)
    y = jax.nn.sigmoid(_dot(h, w2_ref[:, g0:g0 + d])) * _dot(hm, wpm_ref[...])

    y = y + jax.nn.sigmoid(_dot(h, w2_ref[:, g0 + d:g0 + 2 * d])) * _dot(hna_ref[...], wpn_ref[...])

    qn = q_mem * lax.rsqrt(_split_dot(q_mem * q_mem, gm64_ref[...]) + EPS) * gmq_ref[...]
    qn = (qn * (MEM_HEAD_DIM ** -0.5 * LOG2E)).astype(BF16)
    att = _masked_attention(_stack_heads(qn, MEM_HEADS, MEM_HEAD_DIM), km_ref[...], vm_ref[...], None)
    h_mem = _unstack_heads(att, MEM_HEADS, MEM_HEAD_DIM).astype(BF16)
    y = y + jax.nn.sigmoid(_dot(h, w2_ref[:, g0 + 2 * d:g0 + 3 * d])) * _dot(h_mem, wpx_ref[...])

    o_ref[...] = x + _dot(y.astype(BF16), wout_ref[...])


def _merge(x2, hf, hb, hna, k_mem, v_mem, g_mix, w2, g_mlstm, gmq, wpm, wpn, wpx, wout, tm, s):
    n, d = x2.shape
    m = k_mem.shape[1]
    per_b = s // tm
    row = lambda width: pl.BlockSpec((tm, width), lambda i: (i, 0))
    memblk = pl.BlockSpec((None, m, MEM_W), lambda i: (i // per_b, 0, 0))
    return pl.pallas_call(
        _merge_kernel,
        grid=(n // tm,),
        in_specs=[row(d), row(V_W), row(V_W), row(NA_W), memblk, memblk,
                  _const_spec((1, d)), _const_spec(w2.shape), _const_spec((1, V_W)),
                  _const_spec((1, MEM_W)), _const_spec((MEM_W, MEM_W)),
                  _const_spec(wpm.shape), _const_spec(wpn.shape), _const_spec(wpx.shape),
                  _const_spec(wout.shape)],
        out_specs=row(d),
        out_shape=jax.ShapeDtypeStruct((n, d), F32),
        compiler_params=pltpu.CompilerParams(dimension_semantics=("parallel",),
                                             vmem_limit_bytes=VMEM_LIMIT),
        name="merge",
    )(x2, hf, hb, hna, k_mem, v_mem, g_mix, w2, g_mlstm, gmq,
      _group_mean_matrix(MEM_W, MEM_HEAD_DIM), wpm, wpn, wpx, wout)


def _ffn_kernel(x_ref, g_ref, wu_ref, wd_ref, o_ref, *, n_chunks):
    x = x_ref[...]
    h = _rms_rows(x, g_ref[...]).astype(BF16)
    ck = wu_ref.shape[1] // n_chunks
    acc = x
    for j in range(n_chunks):
        u = jnp.maximum(_dot(h, wu_ref[:, j * ck:(j + 1) * ck]), 0.0)
        acc = acc + _dot((u * u).astype(BF16), wd_ref[j * ck:(j + 1) * ck, :])
    o_ref[...] = acc


def _ffn(x2, g_ffn, w_up, w_down, tm):
    n, d = x2.shape
    row = pl.BlockSpec((tm, d), lambda i: (i, 0))
    return pl.pallas_call(
        functools.partial(_ffn_kernel, n_chunks=w_up.shape[1] // d),
        grid=(n // tm,),
        in_specs=[row, _const_spec((1, d)), _const_spec(w_up.shape), _const_spec(w_down.shape)],
        out_specs=row,
        out_shape=jax.ShapeDtypeStruct((n, d), F32),
        compiler_params=pltpu.CompilerParams(dimension_semantics=("parallel",),
                                             vmem_limit_bytes=VMEM_LIMIT),
        name="ffn",
    )(x2, g_ffn, w_up, w_down)


def _layer(x, mem, g_mix, w_in, conv_w, conv_b, b_igate, b_fgate, g_mlstm, w_proj_mlstm,
           g_na_q, g_na_k, rpb, w_proj_na, g_mem, w_mem_kv, g_mem_q, g_mem_k,
           w_proj_mem, w_out, g_ffn, w_up, w_down):
    bsz, s, d = x.shape
    n = bsz * s
    tm = min(ROW_TILE, s)
    x2 = x.reshape(n, d)

    o_qk, o_v = 0, 2 * QK_W
    o_o = o_v + V_W
    o_i = o_o + V_W
    o_f = o_i + 2 * MLSTM_HEADS
    o_na = o_f + 2 * MLSTM_HEADS
    o_qm = o_na + 3 * NA_W
    o_g = o_qm + MEM_W
    gate_pad = ((0, 0), (0, GATE_PAD - N_GATES // 2))
    w_b = w_in.astype(BF16)
    w_gi = jnp.pad(w_b[:, o_i:o_f], gate_pad)
    w_gf = jnp.pad(w_b[:, o_f:o_na], gate_pad)
    bias_i = jnp.pad(b_igate.reshape(1, -1).astype(F32), gate_pad)
    bias_f = jnp.pad(b_fgate.reshape(1, -1).astype(F32), gate_pad)
    w1 = jnp.concatenate([w_b[:, o_qk:o_o], w_gi, w_gf, w_b[:, o_na:o_qm]], axis=1)
    w2 = jnp.concatenate([w_b[:, o_o:o_i], w_b[:, o_qm:o_g], w_b[:, o_g:]], axis=1)
    row = lambda a: a.reshape(1, -1).astype(F32)

    qk, v, gi, gf, qn, kn, vn = _proj_in(x2, row(g_mix), w1, row(jnp.tile(g_na_q, NA_HEADS)),
                                         row(jnp.tile(g_na_k, NA_HEADS)), conv_w.astype(F32),
                                         row(conv_b), tm, s)
    k_mem, v_mem = _mem_kv(mem, row(g_mem), w_mem_kv.astype(BF16), row(jnp.tile(g_mem_k, MEM_HEADS)))

    L = min(MLSTM_CHUNK, s)
    hf, hb = _mlstm(qk.reshape(bsz, s, -1), v.reshape(bsz, s, -1), gi.reshape(bsz, s, -1),
                    gf.reshape(bsz, s, -1), bias_i, bias_f, L, 2 if bsz % 2 == 0 else 1)
    hna = _natten(qn.reshape(bsz, s, -1), kn.reshape(bsz, s, -1), vn.reshape(bsz, s, -1),
                  _na_bias_table(rpb), 8 if (s // GRID_W) % 8 == 0 else 1)

    x1 = _merge(x2, hf.reshape(n, -1), hb.reshape(n, -1), hna.reshape(n, -1), k_mem, v_mem,
                row(g_mix), w2, row(g_mlstm), row(jnp.tile(g_mem_q, MEM_HEADS)),
                w_proj_mlstm.astype(BF16), w_proj_na.astype(BF16), w_proj_mem.astype(BF16),
                w_out.astype(BF16), tm, s)
    out = _ffn(x1, row(g_ffn), w_up.astype(BF16), w_down.astype(BF16), tm)
    return out.reshape(bsz, s, d)


def kernel(x, mem, g_mix, w_in, conv_w, conv_b, b_igate, b_fgate, g_mlstm, w_proj_mlstm,
           g_na_q, g_na_k, rpb, w_proj_na, g_mem, w_mem_kv, g_mem_q, g_mem_k,
           w_proj_mem, w_out, g_ffn, w_up, w_down):
    for l in range(g_mix.shape[0]):
        x = _layer(x, mem, g_mix[l], w_in[l], conv_w[l], conv_b[l], b_igate[l], b_fgate[l],
                   g_mlstm[l], w_proj_mlstm[l], g_na_q[l], g_na_k[l], rpb[l], w_proj_na[l],
                   g_mem[l], w_mem_kv[l], g_mem_q[l], g_mem_k[l], w_proj_mem[l], w_out[l],
                   g_ffn[l], w_up[l], w_down[l])
    return x
```

```python
import functools

import numpy as np
import jax
import jax.numpy as jnp
from jax import lax
from jax.experimental import pallas as pl
from jax.experimental.pallas import tpu as pltpu

GRID_W = 64
MLSTM_HEADS = 4
MLSTM_QK_DIM = 64
MLSTM_V_DIM = 128
MLSTM_CONV = 5
NA_HEADS = 8
NA_HEAD_DIM = 32
NA_WIN_ROWS = 8
NA_WIN_COLS = 16
MEM_HEADS = 4
MEM_HEAD_DIM = 64
N_BRANCH = 3
EPS = 1e-6

QK_W = MLSTM_HEADS * MLSTM_QK_DIM
V_W = MLSTM_HEADS * MLSTM_V_DIM
NA_W = NA_HEADS * NA_HEAD_DIM
MEM_W = MEM_HEADS * MEM_HEAD_DIM
N_GATES = 4 * MLSTM_HEADS
LANES = 128
SUBLANES = 8
HALO = SUBLANES
GATE_PAD = LANES
NEG_BIG = -1e30
LOG2E = 1.4426950408889634

ROW_TILE = 1024
MLSTM_CHUNK = 512
VMEM_LIMIT = 48 * 1024 * 1024

BF16 = jnp.bfloat16
F32 = jnp.float32


def _dot(a, b):
    return jnp.dot(a, b, preferred_element_type=F32)


def _dot_nt(a, b):
    return lax.dot_general(a, b, (((1,), (1,)), ((), ())), preferred_element_type=F32)


def _dot_tn(a, b):
    return lax.dot_general(a, b, (((0,), (0,)), ((), ())), preferred_element_type=F32)


def _split_dot(a, m_bf16):
    hi = a.astype(BF16)
    lo = (a - hi.astype(F32)).astype(BF16)
    return _dot(hi, m_bf16) + _dot(lo, m_bf16)


def _rms_rows(x, g):
    ms = jnp.mean(x * x, axis=-1, keepdims=True)
    return x * lax.rsqrt(ms + EPS) * g


def _group_mean_matrix(width, group):
    idx = np.arange(width) // group
    return jnp.asarray((idx[:, None] == idx[None, :]).astype(np.float32) / group, dtype=BF16)


def _const_spec(shape):
    nd = len(shape)
    return pl.BlockSpec(shape, lambda *_: (0,) * nd, pipeline_mode=pl.Buffered(1))


def _proj_in_kernel(x_ref, xp_ref, xn_ref, g_ref, w_ref, gm_ref, gq_ref, gk_ref, cw_ref, cb_ref,
                    qk_ref, v_ref, gi_ref, gf_ref, qn_ref, kn_ref, vn_ref, *, tiles_per_seq):
    i = pl.program_id(0)
    tm = x_ref.shape[0]
    g = g_ref[...]
    h = _rms_rows(x_ref[...], g).astype(BF16)
    pos = i % tiles_per_seq
    hp = jnp.where(pos > 0, _rms_rows(xp_ref[...], g), 0.0).astype(BF16)
    hn = jnp.where(pos < tiles_per_seq - 1, _rms_rows(xn_ref[...], g), 0.0).astype(BF16)
    o = 0
    z = _dot(jnp.concatenate([hp, h, hn], axis=0), w_ref[:, o:o + 2 * QK_W]); o += 2 * QK_W
    cw = cw_ref[...]
    acc = jnp.zeros((tm, 2 * QK_W), F32) + cb_ref[...]
    for j in range(MLSTM_CONV):
        d = j - MLSTM_CONV // 2
        tap = z if d == 0 else pltpu.roll(z, (-d) % z.shape[0], 0)
        acc = acc + tap[HALO:HALO + tm, :] * cw[j:j + 1, :]
    act = acc * jax.nn.sigmoid(acc)
    lane = lax.broadcasted_iota(jnp.int32, act.shape, 1)
    qk_ref[...] = jnp.where(lane < QK_W, act * (MLSTM_QK_DIM ** -0.5), act).astype(BF16)
    v_ref[...] = _dot(h, w_ref[:, o:o + V_W]).astype(BF16); o += V_W
    gi_ref[...] = _dot(h, w_ref[:, o:o + GATE_PAD]); o += GATE_PAD
    gf_ref[...] = _dot(h, w_ref[:, o:o + GATE_PAD]); o += GATE_PAD
    q = _dot(h, w_ref[:, o:o + NA_W]); o += NA_W
    k = _dot(h, w_ref[:, o:o + NA_W]); o += NA_W
    vn_ref[...] = _dot(h, w_ref[:, o:o + NA_W]).astype(BF16)
    gm = gm_ref[...]
    qn = q * lax.rsqrt(_split_dot(q * q, gm) + EPS) * gq_ref[...]
    qn_ref[...] = (qn * (NA_HEAD_DIM ** -0.5 * LOG2E)).astype(BF16)
    kn_ref[...] = (k * lax.rsqrt(_split_dot(k * k, gm) + EPS) * gk_ref[...]).astype(BF16)


def _proj_in(x2, g_mix, w1, gq, gk, conv_w, conv_b, tm, s):
    n, d = x2.shape
    w_cols = w1.shape[1]
    hb = tm // HALO
    row = lambda width: pl.BlockSpec((tm, width), lambda i: (i, 0))
    prev = pl.BlockSpec((HALO, d), lambda i: (jnp.maximum(i * hb - 1, 0), 0))
    nxt = pl.BlockSpec((HALO, d), lambda i: (jnp.minimum((i + 1) * hb, n // HALO - 1), 0))
    return pl.pallas_call(
        functools.partial(_proj_in_kernel, tiles_per_seq=s // tm),
        grid=(n // tm,),
        in_specs=[row(d), prev, nxt, _const_spec((1, d)), _const_spec((d, w_cols)),
                  _const_spec((NA_W, NA_W)), _const_spec((1, NA_W)), _const_spec((1, NA_W)),
                  _const_spec((MLSTM_CONV, 2 * QK_W)), _const_spec((1, 2 * QK_W))],
        out_specs=[row(2 * QK_W), row(V_W), row(GATE_PAD), row(GATE_PAD), row(NA_W), row(NA_W), row(NA_W)],
        out_shape=[jax.ShapeDtypeStruct((n, 2 * QK_W), BF16),
                   jax.ShapeDtypeStruct((n, V_W), BF16),
                   jax.ShapeDtypeStruct((n, GATE_PAD), F32),
                   jax.ShapeDtypeStruct((n, GATE_PAD), F32),
                   jax.ShapeDtypeStruct((n, NA_W), BF16),
                   jax.ShapeDtypeStruct((n, NA_W), BF16),
                   jax.ShapeDtypeStruct((n, NA_W), BF16)],
        compiler_params=pltpu.CompilerParams(dimension_semantics=("parallel",),
                                             vmem_limit_bytes=VMEM_LIMIT),
        name="proj_in",
    )(x2, x2, x2, g_mix, w1, _group_mean_matrix(NA_W, NA_HEAD_DIM), gq, gk, conv_w, conv_b)


def _mem_kv_kernel(mem_ref, g_ref, w_ref, gm_ref, gk_ref, k_ref, v_ref):
    h = _rms_rows(mem_ref[...], g_ref[...]).astype(BF16)
    k = _dot(h, w_ref[:, :MEM_W])
    v_ref[...] = _dot(h, w_ref[:, MEM_W:]).astype(BF16)
    k_ref[...] = (k * lax.rsqrt(_split_dot(k * k, gm_ref[...]) + EPS) * gk_ref[...]).astype(BF16)


def _mem_kv(mem, g_mem, w_kv, gk):
    b, m, d = mem.shape
    blk = lambda width: pl.BlockSpec((None, m, width), lambda i: (i, 0, 0))
    return pl.pallas_call(
        _mem_kv_kernel,
        grid=(b,),
        in_specs=[blk(d), _const_spec((1, d)), _const_spec((d, 2 * MEM_W)),
                  _const_spec((MEM_W, MEM_W)), _const_spec((1, MEM_W))],
        out_specs=[blk(MEM_W), blk(MEM_W)],
        out_shape=[jax.ShapeDtypeStruct((b, m, MEM_W), BF16)] * 2,
        compiler_params=pltpu.CompilerParams(dimension_semantics=("parallel",),
                                             vmem_limit_bytes=VMEM_LIMIT),
        name="mem_kv",
    )(mem, g_mem, w_kv, _group_mean_matrix(MEM_W, MEM_HEAD_DIM), gk)


def _log_sigmoid(x):
    return jnp.minimum(x, 0.0) - jnp.log1p(jnp.exp(-jnp.abs(x)))


def _cummax_rows(x, reverse):
    n = x.shape[0]
    row = lax.broadcasted_iota(jnp.int32, x.shape, 0)
    sh = 1
    while sh < n:
        if reverse:
            shifted, ok = pltpu.roll(x, n - sh, 0), row < n - sh
        else:
            shifted, ok = pltpu.roll(x, sh, 0), row >= sh
        x = jnp.where(ok, jnp.maximum(x, shifted), x)
        sh *= 2
    return x


def _mlstm_gates(reverse, ig, fpre, m_st):
    L = ig.shape[0]
    last = 0 if reverse else L - 1
    r_i = lax.broadcasted_iota(jnp.int32, (L, L), 0)
    c_i = lax.broadcasted_iota(jnp.int32, (L, L), 1)
    causal = (c_i >= r_i) if reverse else (c_i <= r_i)
    lf = _log_sigmoid(fpre)
    l1 = lf.astype(BF16)
    e1 = lf - l1.astype(F32)
    l2 = e1.astype(BF16)
    l3 = (e1 - l2.astype(F32)).astype(BF16)
    bb = _dot(jnp.where(causal, 1.0, 0.0).astype(BF16), jnp.concatenate([l1, l2, l3], axis=1))
    b = bb[:, :LANES] + bb[:, LANES:2 * LANES] + bb[:, 2 * LANES:]
    a = ig - b
    big_m = jnp.maximum(m_st, _cummax_rows(a, reverse))
    a2 = a * LOG2E
    m2 = big_m * LOG2E
    return dict(
        causal=causal, last=last,
        wint=jnp.exp(m_st - big_m),
        eneg=jnp.exp(-(b + big_m)),
        m2=m2,
        a2_rows=a2.T,
        ws=jnp.exp2(a2 - m2[last:last + 1, :]),
        m_next=(b + big_m)[last:last + 1, :],
    )


def _mlstm_chains(d, g, qk, v, states):
    L = qk.shape[0]
    H = MLSTM_HEADS
    causal, last = g["causal"], g["last"]
    lane = lax.broadcasted_iota(jnp.int32, (L, LANES), 1)
    nums, new_states = [], []
    den_all = jnp.zeros((L, LANES), F32)
    for hd in range(H):
        c = d * H + hd
        pair = hd // 2
        in_head = (lane // MLSTM_QK_DIM) == (hd % 2)
        q_h = jnp.where(in_head, qk[:, pair * LANES:(pair + 1) * LANES], jnp.zeros((), BF16))
        k_p = qk[:, QK_W + pair * LANES:QK_W + (pair + 1) * LANES]
        v_ext = jnp.concatenate([v[:, hd * LANES:(hd + 1) * LANES],
                                 jnp.where(lane == c, 1.0, 0.0).astype(BF16)], axis=1)
        st = states[hd]
        decay_log = g["a2_rows"][c:c + 1, :] - g["m2"][:, c:c + 1]
        p = jnp.exp2(jnp.where(causal, decay_log, -jnp.inf))
        s_mat = _dot_nt(q_h, k_p) * p
        q_inter = q_h.astype(F32) * g["wint"][:, c:c + 1]
        lhs = jnp.concatenate([s_mat.astype(BF16), q_inter.astype(BF16)], axis=1)
        rhs = jnp.concatenate([v_ext, st.astype(BF16)], axis=0)
        num_ext = _dot(lhs, rhs)
        nums.append(num_ext[:, :LANES])
        den_all = den_all + num_ext[:, LANES:]
        kw = (k_p.astype(F32) * g["ws"][:, c:c + 1]).astype(BF16)
        new_states.append(g["wint"][last:last + 1, c:c + 1] * st + _dot_tn(kw, v_ext))
    r_all = 1.0 / jnp.maximum(jnp.abs(den_all), g["eneg"])
    h = jnp.concatenate([nums[hd] * r_all[:, d * H + hd:d * H + hd + 1] for hd in range(H)], axis=1)
    return h, new_states


def _mlstm_kernel(qkf_ref, qkb_ref, vf_ref, vb_ref, gif_ref, gib_ref, gff_ref, gfb_ref, bi_ref, bf_ref,
                  hf_ref, hb_ref, st_ref, m_ref):
    H = MLSTM_HEADS

    @pl.when(pl.program_id(1) == 0)
    def _():
        st_ref[...] = jnp.zeros_like(st_ref)
        m_ref[...] = jnp.full(m_ref.shape, -jnp.inf, F32)

    bi, bf = bi_ref[...], bf_ref[...]
    results = []
    for e in range(qkf_ref.shape[0]):
        g_f = _mlstm_gates(False, gif_ref[e] + bi, gff_ref[e] + bf, m_ref[2 * e, 0:1, :])
        g_b = _mlstm_gates(True, gib_ref[e] + bi, gfb_ref[e] + bf, m_ref[2 * e + 1, 0:1, :])
        base = 2 * H * e
        h_f, st_f = _mlstm_chains(0, g_f, qkf_ref[e], vf_ref[e], [st_ref[base + c] for c in range(H)])
        h_b, st_b = _mlstm_chains(1, g_b, qkb_ref[e], vb_ref[e], [st_ref[base + H + c] for c in range(H)])
        results.append((h_f, h_b, st_f + st_b, g_f["m_next"], g_b["m_next"]))
    for e, (h_f, h_b, sts, m_f, m_b) in enumerate(results):
        hf_ref[e] = h_f
        hb_ref[e] = h_b
        for c, st in enumerate(sts):
            st_ref[2 * H * e + c] = st
        m_ref[2 * e] = jnp.broadcast_to(m_f, m_ref.shape[1:])
        m_ref[2 * e + 1] = jnp.broadcast_to(m_b, m_ref.shape[1:])


def _mlstm(qk3, v3, gi3, gf3, bias_i, bias_f, L, nb):
    b, s, _ = qk3.shape
    nc = s // L
    fwd = lambda bi, c: (bi, c, 0)
    bwd = lambda bi, c: (bi, nc - 1 - c, 0)
    blk = lambda width, pos: pl.BlockSpec((nb, L, width), pos)
    return pl.pallas_call(
        _mlstm_kernel,
        grid=(b // nb, nc),
        in_specs=[blk(2 * QK_W, fwd), blk(2 * QK_W, bwd), blk(V_W, fwd), blk(V_W, bwd),
                  blk(GATE_PAD, fwd), blk(GATE_PAD, bwd), blk(GATE_PAD, fwd), blk(GATE_PAD, bwd),
                  _const_spec((1, GATE_PAD)), _const_spec((1, GATE_PAD))],
        out_specs=[blk(V_W, fwd), blk(V_W, bwd)],
        out_shape=[jax.ShapeDtypeStruct((b, s, V_W), F32)] * 2,
        scratch_shapes=[pltpu.VMEM((nb * 2 * MLSTM_HEADS, LANES, 2 * LANES), F32),
                        pltpu.VMEM((nb * 2, SUBLANES, LANES), F32)],
        compiler_params=pltpu.CompilerParams(dimension_semantics=("parallel", "arbitrary"),
                                             vmem_limit_bytes=VMEM_LIMIT),
        name="mlstm",
    )(qk3, qk3, v3, v3, gi3, gi3, gf3, gf3, bias_i, bias_f)


def _stack_heads(q, n_heads, head_dim):
    lane = lax.broadcasted_iota(jnp.int32, q.shape, 1)
    zero = jnp.zeros_like(q)
    return jnp.concatenate([jnp.where(lane // head_dim == h, q, zero) for h in range(n_heads)], axis=0)


def _unstack_heads(o, n_heads, head_dim):
    t = o.shape[0] // n_heads
    lane = lax.broadcasted_iota(jnp.int32, (t, o.shape[1]), 1)
    acc = jnp.zeros((t, o.shape[1]), o.dtype)
    for h in range(n_heads):
        acc = jnp.where(lane // head_dim == h, o[h * t:(h + 1) * t, :], acc)
    return acc


def _masked_attention(q_stacked, k, v, bias):
    sc = _dot_nt(q_stacked, k)
    if bias is not None:
        sc = sc + bias
    p = jnp.exp2(sc - jnp.max(sc, axis=-1, keepdims=True))
    o = _dot(p.astype(BF16), v)
    return o / jnp.sum(p, axis=-1, keepdims=True)


def _natten_kernel(q_ref, k_ref, v_ref, bias_ref, o_ref, *, rows, rows_per_step):
    for j in range(rows_per_step):
        r = pl.program_id(1) * rows_per_step + j
        rs = jnp.clip(r - NA_WIN_ROWS // 2, 0, rows - NA_WIN_ROWS)
        start = pl.multiple_of(rs * GRID_W, GRID_W)
        k_win = k_ref[pl.ds(start, NA_WIN_ROWS * GRID_W), :]
        v_win = v_ref[pl.ds(start, NA_WIN_ROWS * GRID_W), :]
        q = q_ref[j * GRID_W:(j + 1) * GRID_W, :]
        off = NA_WIN_ROWS - 1 - (r - rs)
        lane0 = pl.multiple_of((off // 2) * LANES, LANES)
        bias = bias_ref[off % 2, :, pl.ds(lane0, NA_WIN_ROWS * GRID_W)]
        o = _masked_attention(_stack_heads(q, NA_HEADS, NA_HEAD_DIM), k_win, v_win, bias)
        o_ref[j * GRID_W:(j + 1) * GRID_W, :] = _unstack_heads(o, NA_HEADS, NA_HEAD_DIM).astype(o_ref.dtype)


def _na_bias_kernel(rpb_ref, onehot_ref, mask_ref, o_ref):
    r = rpb_ref[...]
    r1 = r.astype(BF16)
    e1 = r - r1.astype(F32)
    r2 = e1.astype(BF16)
    r3 = (e1 - r2.astype(F32)).astype(BF16)
    oh = onehot_ref[...]
    o_ref[...] = (_dot(r1, oh) + _dot(r2, oh) + _dot(r3, oh)) * LOG2E + mask_ref[...]


def _na_bias_table(rpb):
    n_ro, n_co = 2 * NA_WIN_ROWS - 1, 2 * NA_WIN_COLS - 1
    c = np.arange(GRID_W)[:, None]
    kc = np.arange(GRID_W)[None, :]
    cs = np.clip(c - NA_WIN_COLS // 2, 0, GRID_W - NA_WIN_COLS)
    valid = (kc >= cs) & (kc < cs + NA_WIN_COLS)
    col_off = kc - c + NA_WIN_COLS - 1
    onehot = (np.arange(LANES)[:, None, None] == col_off[None]) & valid[None]
    onehot = jnp.asarray(onehot.reshape(LANES, GRID_W * GRID_W), dtype=BF16)
    mask = jnp.asarray(np.where(valid, 0.0, NEG_BIG).reshape(1, GRID_W * GRID_W), dtype=F32)
    rp = jnp.pad(rpb.astype(F32).reshape(NA_HEADS * n_ro, n_co), ((0, 0), (0, LANES - n_co)))
    toep = pl.pallas_call(
        _na_bias_kernel,
        out_shape=jax.ShapeDtypeStruct((NA_HEADS * n_ro, GRID_W * GRID_W), F32),
        name="na_bias",
    )(rp, onehot, mask)
    toep = toep.reshape(NA_HEADS, n_ro, GRID_W, GRID_W).transpose(0, 2, 1, 3)
    toep = toep.reshape(NA_HEADS * GRID_W, n_ro * GRID_W)
    width = (n_ro - 1) * GRID_W
    return jnp.stack([toep[:, :width], toep[:, GRID_W:]])


def _natten(qn, kn, vn, bias_tab, rows_per_step):
    b, s, _ = qn.shape
    rows = s // GRID_W
    full = pl.BlockSpec((None, s, NA_W), lambda bi, r: (bi, 0, 0))
    tile = pl.BlockSpec((None, rows_per_step * GRID_W, NA_W), lambda bi, r: (bi, r, 0))
    return pl.pallas_call(
        functools.partial(_natten_kernel, rows=rows, rows_per_step=rows_per_step),
        grid=(b, rows // rows_per_step),
        in_specs=[tile, full, full, _const_spec(bias_tab.shape)],
        out_specs=tile,
        out_shape=jax.ShapeDtypeStruct((b, s, NA_W), BF16),
        compiler_params=pltpu.CompilerParams(dimension_semantics=("parallel", "arbitrary"),
                                             vmem_limit_bytes=VMEM_LIMIT),
        name="natten",
    )(qn, kn, vn, bias_tab)


def _merge_kernel(x_ref, hf_ref, hb_ref, hna_ref, km_ref, vm_ref,
                  g_ref, w2_ref, gml_ref, gmq_ref, gm64_ref,
                  wpm_ref, wpn_ref, wpx_ref, wout_ref, o_ref):
    x = x_ref[...]
    d = x.shape[1]
    h = _rms_rows(x, g_ref[...]).astype(BF16)
    o_pre = _dot(h, w2_ref[:, :V_W])
    q_mem = _dot(h, w2_ref[:, V_W:V_W + MEM_W])
    g0 = V_W + MEM_W

    hm = hf_ref[...] + hb_ref[...]
    gml = gml_ref[...]
    parts = []
    for hd in range(MLSTM_HEADS):
        sl = slice(hd * MLSTM_V_DIM, (hd + 1) * MLSTM_V_DIM)
        parts.append(_rms_rows(hm[:, sl], gml[:, sl]))
    hm = (jnp.concatenate(parts, axis=1) * jax.nn.sigmoid(o_pre)).astype(BF16)
    y = jax.nn.sigmoid(_dot(h, w2_ref[:, g0:g0 + d])) * _dot(hm, wpm_ref[...])

    y = y + jax.nn.sigmoid(_dot(h, w2_ref[:, g0 + d:g0 + 2 * d])) * _dot(hna_ref[...], wpn_ref[...])

    qn = q_mem * lax.rsqrt(_split_dot(q_mem * q_mem, gm64_ref[...]) + EPS) * gmq_ref[...]
    qn = (qn * (MEM_HEAD_DIM ** -0.5 * LOG2E)).astype(BF16)
    att = _masked_attention(_stack_heads(qn, MEM_HEADS, MEM_HEAD_DIM), km_ref[...], vm_ref[...], None)
    h_mem = _unstack_heads(att, MEM_HEADS, MEM_HEAD_DIM).astype(BF16)
    y = y + jax.nn.sigmoid(_dot(h, w2_ref[:, g0 + 2 * d:g0 + 3 * d])) * _dot(h_mem, wpx_ref[...])

    o_ref[...] = x + _dot(y.astype(BF16), wout_ref[...])


def _merge(x2, hf, hb, hna, k_mem, v_mem, g_mix, w2, g_mlstm, gmq, wpm, wpn, wpx, wout, tm, s):
    n, d = x2.shape
    m = k_mem.shape[1]
    per_b = s // tm
    row = lambda width: pl.BlockSpec((tm, width), lambda i: (i, 0))
    memblk = pl.BlockSpec((None, m, MEM_W), lambda i: (i // per_b, 0, 0))
    return pl.pallas_call(
        _merge_kernel,
        grid=(n // tm,),
        in_specs=[row(d), row(V_W), row(V_W), row(NA_W), memblk, memblk,
                  _const_spec((1, d)), _const_spec(w2.shape), _const_spec((1, V_W)),
                  _const_spec((1, MEM_W)), _const_spec((MEM_W, MEM_W)),
                  _const_spec(wpm.shape), _const_spec(wpn.shape), _const_spec(wpx.shape),
                  _const_spec(wout.shape)],
        out_specs=row(d),
        out_shape=jax.ShapeDtypeStruct((n, d), F32),
        compiler_params=pltpu.CompilerParams(dimension_semantics=("parallel",),
                                             vmem_limit_bytes=VMEM_LIMIT),
        name="merge",
    )(x2, hf, hb, hna, k_mem, v_mem, g_mix, w2, g_mlstm, gmq,
      _group_mean_matrix(MEM_W, MEM_HEAD_DIM), wpm, wpn, wpx, wout)


def _ffn_kernel(x_ref, g_ref, wu_ref, wd_ref, o_ref, *, n_chunks):
    x = x_ref[...]
    h = _rms_rows(x, g_ref[...]).astype(BF16)
    ck = wu_ref.shape[1] // n_chunks
    acc = x
    for j in range(n_chunks):
        u = jnp.maximum(_dot(h, wu_ref[:, j * ck:(j + 1) * ck]), 0.0)
        acc = acc + _dot((u * u).astype(BF16), wd_ref[j * ck:(j + 1) * ck, :])
    o_ref[...] = acc


def _ffn(x2, g_ffn, w_up, w_down, tm):
    n, d = x2.shape
    row = pl.BlockSpec((tm, d), lambda i: (i, 0))
    return pl.pallas_call(
        functools.partial(_ffn_kernel, n_chunks=w_up.shape[1] // d),
        grid=(n // tm,),
        in_specs=[row, _const_spec((1, d)), _const_spec(w_up.shape), _const_spec(w_down.shape)],
        out_specs=row,
        out_shape=jax.ShapeDtypeStruct((n, d), F32),
        compiler_params=pltpu.CompilerParams(dimension_semantics=("parallel",),
                                             vmem_limit_bytes=VMEM_LIMIT),
        name="ffn",
    )(x2, g_ffn, w_up, w_down)


def _layer(x, mem, g_mix, w_in, conv_w, conv_b, b_igate, b_fgate, g_mlstm, w_proj_mlstm,
           g_na_q, g_na_k, rpb, w_proj_na, g_mem, w_mem_kv, g_mem_q, g_mem_k,
           w_proj_mem, w_out, g_ffn, w_up, w_down):
    bsz, s, d = x.shape
    n = bsz * s
    tm = min(ROW_TILE, s)
    x2 = x.reshape(n, d)

    o_qk, o_v = 0, 2 * QK_W
    o_o = o_v + V_W
    o_i = o_o + V_W
    o_f = o_i + 2 * MLSTM_HEADS
    o_na = o_f + 2 * MLSTM_HEADS
    o_qm = o_na + 3 * NA_W
    o_g = o_qm + MEM_W
    gate_pad = ((0, 0), (0, GATE_PAD - N_GATES // 2))
    w_b = w_in.astype(BF16)
    w_gi = jnp.pad(w_b[:, o_i:o_f], gate_pad)
    w_gf = jnp.pad(w_b[:, o_f:o_na], gate_pad)
    bias_i = jnp.pad(b_igate.reshape(1, -1).astype(F32), gate_pad)
    bias_f = jnp.pad(b_fgate.reshape(1, -1).astype(F32), gate_pad)
    w1 = jnp.concatenate([w_b[:, o_qk:o_o], w_gi, w_gf, w_b[:, o_na:o_qm]], axis=1)
    w2 = jnp.concatenate([w_b[:, o_o:o_i], w_b[:, o_qm:o_g], w_b[:, o_g:]], axis=1)
    row = lambda a: a.reshape(1, -1).astype(F32)

    qk, v, gi, gf, qn, kn, vn = _proj_in(x2, row(g_mix), w1, row(jnp.tile(g_na_q, NA_HEADS)),
                                         row(jnp.tile(g_na_k, NA_HEADS)), conv_w.astype(F32),
                                         row(conv_b), tm, s)
    k_mem, v_mem = _mem_kv(mem, row(g_mem), w_mem_kv.astype(BF16), row(jnp.tile(g_mem_k, MEM_HEADS)))

    L = min(MLSTM_CHUNK, s)
    hf, hb = _mlstm(qk.reshape(bsz, s, -1), v.reshape(bsz, s, -1), gi.reshape(bsz, s, -1),
                    gf.reshape(bsz, s, -1), bias_i, bias_f, L, 2 if bsz % 2 == 0 else 1)
    hna = _natten(qn.reshape(bsz, s, -1), kn.reshape(bsz, s, -1), vn.reshape(bsz, s, -1),
                  _na_bias_table(rpb), 16 if (s // GRID_W) % 16 == 0 else 1)

    x1 = _merge(x2, hf.reshape(n, -1), hb.reshape(n, -1), hna.reshape(n, -1), k_mem, v_mem,
                row(g_mix), w2, row(g_mlstm), row(jnp.tile(g_mem_q, MEM_HEADS)),
                w_proj_mlstm.astype(BF16), w_proj_na.astype(BF16), w_proj_mem.astype(BF16),
                w_out.astype(BF16), tm, s)
    out = _ffn(x1, row(g_ffn), w_up.astype(BF16), w_down.astype(BF16), tm)
    return out.reshape(bsz, s, d)


def kernel(x, mem, g_mix, w_in, conv_w, conv_b, b_igate, b_fgate, g_mlstm, w_proj_mlstm,
           g_na_q, g_na_k, rpb, w_proj_na, g_mem, w_mem_kv, g_mem_q, g_mem_k,
           w_proj_mem, w_out, g_ffn, w_up, w_down):
    for l in range(g_mix.shape[0]):
        x = _layer(x, mem, g_mix[l], w_in[l], conv_w[l], conv_b[l], b_igate[l], b_fgate[l],
                   g_mlstm[l], w_proj_mlstm[l], g_na_q[l], g_na_k[l], rpb[l], w_proj_na[l],
                   g_mem[l], w_mem_kv[l], g_mem_q[l], g_mem_k[l], w_proj_mem[l], w_out[l],
                   g_ffn[l], w_up[l], w_down[l])
    return x
```

```python
import functools

import numpy as np
import jax
import jax.numpy as jnp
from jax import lax
from jax.experimental import pallas as pl
from jax.experimental.pallas import tpu as pltpu

GRID_W = 64
MLSTM_HEADS = 4
MLSTM_QK_DIM = 64
MLSTM_V_DIM = 128
MLSTM_CONV = 5
NA_HEADS = 8
NA_HEAD_DIM = 32
NA_WIN_ROWS = 8
NA_WIN_COLS = 16
MEM_HEADS = 4
MEM_HEAD_DIM = 64
N_BRANCH = 3
EPS = 1e-6

QK_W = MLSTM_HEADS * MLSTM_QK_DIM
V_W = MLSTM_HEADS * MLSTM_V_DIM
NA_W = NA_HEADS * NA_HEAD_DIM
MEM_W = MEM_HEADS * MEM_HEAD_DIM
N_GATES = 4 * MLSTM_HEADS
LANES = 128
SUBLANES = 8
HALO = SUBLANES
GATE_PAD = LANES
NEG_BIG = -1e30
LOG2E = 1.4426950408889634

ROW_TILE = 1024
MLSTM_CHUNK = 512
VMEM_LIMIT = 48 * 1024 * 1024

BF16 = jnp.bfloat16
F32 = jnp.float32


def _dot(a, b):
    return jnp.dot(a, b, preferred_element_type=F32)


def _dot_nt(a, b):
    return lax.dot_general(a, b, (((1,), (1,)), ((), ())), preferred_element_type=F32)


def _dot_tn(a, b):
    return lax.dot_general(a, b, (((0,), (0,)), ((), ())), preferred_element_type=F32)


def _group_mean_sq(a, m_bf16):
    return _dot((a * a).astype(BF16), m_bf16)


def _rms_rows(x, g):
    ms = jnp.mean(x * x, axis=-1, keepdims=True)
    return x * lax.rsqrt(ms + EPS) * g


def _group_mean_matrix(width, group):
    idx = np.arange(width) // group
    return jnp.asarray((idx[:, None] == idx[None, :]).astype(np.float32) / group, dtype=BF16)


def _const_spec(shape):
    nd = len(shape)
    return pl.BlockSpec(shape, lambda *_: (0,) * nd, pipeline_mode=pl.Buffered(1))


def _proj_in_kernel(x_ref, xp_ref, xn_ref, g_ref, w_ref, gm_ref, gq_ref, gk_ref, cw_ref, cb_ref,
                    qk_ref, v_ref, gi_ref, gf_ref, qn_ref, kn_ref, vn_ref, *, tiles_per_seq):
    i = pl.program_id(0)
    tm = x_ref.shape[0]
    g = g_ref[...]
    h = _rms_rows(x_ref[...], g).astype(BF16)
    pos = i % tiles_per_seq
    hp = jnp.where(pos > 0, _rms_rows(xp_ref[...], g), 0.0).astype(BF16)
    hn = jnp.where(pos < tiles_per_seq - 1, _rms_rows(xn_ref[...], g), 0.0).astype(BF16)
    o = 0
    z = _dot(jnp.concatenate([hp, h, hn], axis=0), w_ref[:, o:o + 2 * QK_W]); o += 2 * QK_W
    cw = cw_ref[...]
    acc = jnp.zeros((tm, 2 * QK_W), F32) + cb_ref[...]
    for j in range(MLSTM_CONV):
        d = j - MLSTM_CONV // 2
        tap = z if d == 0 else pltpu.roll(z, (-d) % z.shape[0], 0)
        acc = acc + tap[HALO:HALO + tm, :] * cw[j:j + 1, :]
    act = acc * jax.nn.sigmoid(acc)
    lane = lax.broadcasted_iota(jnp.int32, act.shape, 1)
    qk_ref[...] = jnp.where(lane < QK_W, act * (MLSTM_QK_DIM ** -0.5), act).astype(BF16)
    v_ref[...] = _dot(h, w_ref[:, o:o + V_W]).astype(BF16); o += V_W
    gi_ref[...] = _dot(h, w_ref[:, o:o + GATE_PAD]); o += GATE_PAD
    gf_ref[...] = _dot(h, w_ref[:, o:o + GATE_PAD]); o += GATE_PAD
    q = _dot(h, w_ref[:, o:o + NA_W]); o += NA_W
    k = _dot(h, w_ref[:, o:o + NA_W]); o += NA_W
    vn_ref[...] = _dot(h, w_ref[:, o:o + NA_W]).astype(BF16)
    gm = gm_ref[...]
    qn = q * lax.rsqrt(_group_mean_sq(q, gm) + EPS) * gq_ref[...]
    qn_ref[...] = (qn * (NA_HEAD_DIM ** -0.5 * LOG2E)).astype(BF16)
    kn_ref[...] = (k * lax.rsqrt(_group_mean_sq(k, gm) + EPS) * gk_ref[...]).astype(BF16)


def _proj_in(x2, g_mix, w1, gq, gk, conv_w, conv_b, tm, s):
    n, d = x2.shape
    w_cols = w1.shape[1]
    hb = tm // HALO
    row = lambda width: pl.BlockSpec((tm, width), lambda i: (i, 0))
    prev = pl.BlockSpec((HALO, d), lambda i: (jnp.maximum(i * hb - 1, 0), 0))
    nxt = pl.BlockSpec((HALO, d), lambda i: (jnp.minimum((i + 1) * hb, n // HALO - 1), 0))
    return pl.pallas_call(
        functools.partial(_proj_in_kernel, tiles_per_seq=s // tm),
        grid=(n // tm,),
        in_specs=[row(d), prev, nxt, _const_spec((1, d)), _const_spec((d, w_cols)),
                  _const_spec((NA_W, NA_W)), _const_spec((1, NA_W)), _const_spec((1, NA_W)),
                  _const_spec((MLSTM_CONV, 2 * QK_W)), _const_spec((1, 2 * QK_W))],
        out_specs=[row(2 * QK_W), row(V_W), row(GATE_PAD), row(GATE_PAD), row(NA_W), row(NA_W), row(NA_W)],
        out_shape=[jax.ShapeDtypeStruct((n, 2 * QK_W), BF16),
                   jax.ShapeDtypeStruct((n, V_W), BF16),
                   jax.ShapeDtypeStruct((n, GATE_PAD), F32),
                   jax.ShapeDtypeStruct((n, GATE_PAD), F32),
                   jax.ShapeDtypeStruct((n, NA_W), BF16),
                   jax.ShapeDtypeStruct((n, NA_W), BF16),
                   jax.ShapeDtypeStruct((n, NA_W), BF16)],
        compiler_params=pltpu.CompilerParams(dimension_semantics=("parallel",),
                                             vmem_limit_bytes=VMEM_LIMIT),
        name="proj_in",
    )(x2, x2, x2, g_mix, w1, _group_mean_matrix(NA_W, NA_HEAD_DIM), gq, gk, conv_w, conv_b)


def _mem_kv_kernel(mem_ref, g_ref, w_ref, gm_ref, gk_ref, k_ref, v_ref):
    h = _rms_rows(mem_ref[...], g_ref[...]).astype(BF16)
    k = _dot(h, w_ref[:, :MEM_W])
    v_ref[...] = _dot(h, w_ref[:, MEM_W:]).astype(BF16)
    k_ref[...] = (k * lax.rsqrt(_group_mean_sq(k, gm_ref[...]) + EPS) * gk_ref[...]).astype(BF16)


def _mem_kv(mem, g_mem, w_kv, gk):
    b, m, d = mem.shape
    blk = lambda width: pl.BlockSpec((None, m, width), lambda i: (i, 0, 0))
    return pl.pallas_call(
        _mem_kv_kernel,
        grid=(b,),
        in_specs=[blk(d), _const_spec((1, d)), _const_spec((d, 2 * MEM_W)),
                  _const_spec((MEM_W, MEM_W)), _const_spec((1, MEM_W))],
        out_specs=[blk(MEM_W), blk(MEM_W)],
        out_shape=[jax.ShapeDtypeStruct((b, m, MEM_W), BF16)] * 2,
        compiler_params=pltpu.CompilerParams(dimension_semantics=("parallel",),
                                             vmem_limit_bytes=VMEM_LIMIT),
        name="mem_kv",
    )(mem, g_mem, w_kv, _group_mean_matrix(MEM_W, MEM_HEAD_DIM), gk)


def _log_sigmoid(x):
    return jnp.minimum(x, 0.0) - jnp.log1p(jnp.exp(-jnp.abs(x)))


def _cummax_rows(x, reverse):
    n = x.shape[0]
    row = lax.broadcasted_iota(jnp.int32, x.shape, 0)
    sh = 1
    while sh < n:
        if reverse:
            shifted, ok = pltpu.roll(x, n - sh, 0), row < n - sh
        else:
            shifted, ok = pltpu.roll(x, sh, 0), row >= sh
        x = jnp.where(ok, jnp.maximum(x, shifted), x)
        sh *= 2
    return x


def _mlstm_gates(reverse, ig, fpre, m_st):
    L = ig.shape[0]
    last = 0 if reverse else L - 1
    r_i = lax.broadcasted_iota(jnp.int32, (L, L), 0)
    c_i = lax.broadcasted_iota(jnp.int32, (L, L), 1)
    causal = (c_i >= r_i) if reverse else (c_i <= r_i)
    lf = _log_sigmoid(fpre)
    l1 = lf.astype(BF16)
    l2 = (lf - l1.astype(F32)).astype(BF16)
    bb = _dot(jnp.where(causal, 1.0, 0.0).astype(BF16), jnp.concatenate([l1, l2], axis=1))
    b = bb[:, :LANES] + bb[:, LANES:]
    a = ig - b
    big_m = jnp.maximum(m_st, _cummax_rows(a, reverse))
    a2 = a * LOG2E
    m2 = big_m * LOG2E
    return dict(
        causal=causal, last=last,
        wint=jnp.exp(m_st - big_m),
        eneg=jnp.exp(-(b + big_m)),
        m2=m2,
        a2_rows=a2.T,
        ws=jnp.exp2(a2 - m2[last:last + 1, :]),
        m_next=(b + big_m)[last:last + 1, :],
    )


def _mlstm_chains(d, g, qk, v, states):
    L = qk.shape[0]
    H = MLSTM_HEADS
    causal, last = g["causal"], g["last"]
    lane = lax.broadcasted_iota(jnp.int32, (L, LANES), 1)
    nums, new_states = [], []
    den_all = jnp.zeros((L, LANES), F32)
    for hd in range(H):
        c = d * H + hd
        pair = hd // 2
        in_head = (lane // MLSTM_QK_DIM) == (hd % 2)
        q_h = jnp.where(in_head, qk[:, pair * LANES:(pair + 1) * LANES], jnp.zeros((), BF16))
        k_p = qk[:, QK_W + pair * LANES:QK_W + (pair + 1) * LANES]
        v_ext = jnp.concatenate([v[:, hd * LANES:(hd + 1) * LANES],
                                 jnp.where(lane == c, 1.0, 0.0).astype(BF16)], axis=1)
        st = states[hd]
        decay_log = g["a2_rows"][c:c + 1, :] - g["m2"][:, c:c + 1]
        p = jnp.exp2(jnp.where(causal, decay_log, -jnp.inf))
        s_mat = _dot_nt(q_h, k_p) * p
        q_inter = q_h.astype(F32) * g["wint"][:, c:c + 1]
        lhs = jnp.concatenate([s_mat.astype(BF16), q_inter.astype(BF16)], axis=1)
        rhs = jnp.concatenate([v_ext, st.astype(BF16)], axis=0)
        num_ext = _dot(lhs, rhs)
        nums.append(num_ext[:, :LANES])
        den_all = den_all + num_ext[:, LANES:]
        kw = (k_p.astype(F32) * g["ws"][:, c:c + 1]).astype(BF16)
        new_states.append(g["wint"][last:last + 1, c:c + 1] * st + _dot_tn(kw, v_ext))
    r_all = 1.0 / jnp.maximum(jnp.abs(den_all), g["eneg"])
    h = jnp.concatenate([nums[hd] * r_all[:, d * H + hd:d * H + hd + 1] for hd in range(H)], axis=1)
    return h, new_states


def _mlstm_kernel(qkf_ref, qkb_ref, vf_ref, vb_ref, gif_ref, gib_ref, gff_ref, gfb_ref, bi_ref, bf_ref,
                  hf_ref, hb_ref, st_ref, m_ref):
    H = MLSTM_HEADS

    @pl.when(pl.program_id(1) == 0)
    def _():
        st_ref[...] = jnp.zeros_like(st_ref)
        m_ref[...] = jnp.full(m_ref.shape, -jnp.inf, F32)

    bi, bf = bi_ref[...], bf_ref[...]
    results = []
    for e in range(qkf_ref.shape[0]):
        g_f = _mlstm_gates(False, gif_ref[e] + bi, gff_ref[e] + bf, m_ref[2 * e, 0:1, :])
        g_b = _mlstm_gates(True, gib_ref[e] + bi, gfb_ref[e] + bf, m_ref[2 * e + 1, 0:1, :])
        base = 2 * H * e
        h_f, st_f = _mlstm_chains(0, g_f, qkf_ref[e], vf_ref[e], [st_ref[base + c] for c in range(H)])
        h_b, st_b = _mlstm_chains(1, g_b, qkb_ref[e], vb_ref[e], [st_ref[base + H + c] for c in range(H)])
        results.append((h_f, h_b, st_f + st_b, g_f["m_next"], g_b["m_next"]))
    for e, (h_f, h_b, sts, m_f, m_b) in enumerate(results):
        hf_ref[e] = h_f
        hb_ref[e] = h_b
        for c, st in enumerate(sts):
            st_ref[2 * H * e + c] = st
        m_ref[2 * e] = jnp.broadcast_to(m_f, m_ref.shape[1:])
        m_ref[2 * e + 1] = jnp.broadcast_to(m_b, m_ref.shape[1:])


def _mlstm(qk3, v3, gi3, gf3, bias_i, bias_f, L, nb):
    b, s, _ = qk3.shape
    nc = s // L
    fwd = lambda bi, c: (bi, c, 0)
    bwd = lambda bi, c: (bi, nc - 1 - c, 0)
    blk = lambda width, pos: pl.BlockSpec((nb, L, width), pos)
    return pl.pallas_call(
        _mlstm_kernel,
        grid=(b // nb, nc),
        in_specs=[blk(2 * QK_W, fwd), blk(2 * QK_W, bwd), blk(V_W, fwd), blk(V_W, bwd),
                  blk(GATE_PAD, fwd), blk(GATE_PAD, bwd), blk(GATE_PAD, fwd), blk(GATE_PAD, bwd),
                  _const_spec((1, GATE_PAD)), _const_spec((1, GATE_PAD))],
        out_specs=[blk(V_W, fwd), blk(V_W, bwd)],
        out_shape=[jax.ShapeDtypeStruct((b, s, V_W), F32)] * 2,
        scratch_shapes=[pltpu.VMEM((nb * 2 * MLSTM_HEADS, LANES, 2 * LANES), F32),
                        pltpu.VMEM((nb * 2, SUBLANES, LANES), F32)],
        compiler_params=pltpu.CompilerParams(dimension_semantics=("parallel", "arbitrary"),
                                             vmem_limit_bytes=VMEM_LIMIT),
        name="mlstm",
    )(qk3, qk3, v3, v3, gi3, gi3, gf3, gf3, bias_i, bias_f)


def _stack_heads(q, n_heads, head_dim):
    lane = lax.broadcasted_iota(jnp.int32, q.shape, 1)
    zero = jnp.zeros_like(q)
    return jnp.concatenate([jnp.where(lane // head_dim == h, q, zero) for h in range(n_heads)], axis=0)


def _unstack_heads(o, n_heads, head_dim):
    t = o.shape[0] // n_heads
    lane = lax.broadcasted_iota(jnp.int32, (t, o.shape[1]), 1)
    acc = jnp.zeros((t, o.shape[1]), o.dtype)
    for h in range(n_heads):
        acc = jnp.where(lane // head_dim == h, o[h * t:(h + 1) * t, :], acc)
    return acc


def _masked_attention(q_stacked, k, v, bias):
    sc = _dot_nt(q_stacked, k)
    if bias is not None:
        sc = sc + bias
    p = jnp.exp2(sc - jnp.max(sc, axis=-1, keepdims=True))
    o = _dot(p.astype(BF16), v)
    return o / jnp.sum(p, axis=-1, keepdims=True)


def _natten_kernel(q_ref, k_ref, v_ref, bias_ref, o_ref, *, rows, rows_per_step):
    for j in range(rows_per_step):
        r = pl.program_id(1) * rows_per_step + j
        rs = jnp.clip(r - NA_WIN_ROWS // 2, 0, rows - NA_WIN_ROWS)
        start = pl.multiple_of(rs * GRID_W, GRID_W)
        k_win = k_ref[pl.ds(start, NA_WIN_ROWS * GRID_W), :]
        v_win = v_ref[pl.ds(start, NA_WIN_ROWS * GRID_W), :]
        q = q_ref[j * GRID_W:(j + 1) * GRID_W, :]
        off = NA_WIN_ROWS - 1 - (r - rs)
        lane0 = pl.multiple_of((off // 2) * LANES, LANES)
        bias = bias_ref[off % 2, :, pl.ds(lane0, NA_WIN_ROWS * GRID_W)]
        o = _masked_attention(_stack_heads(q, NA_HEADS, NA_HEAD_DIM), k_win, v_win, bias)
        o_ref[j * GRID_W:(j + 1) * GRID_W, :] = _unstack_heads(o, NA_HEADS, NA_HEAD_DIM).astype(o_ref.dtype)


def _na_bias_kernel(rpb_ref, onehot_ref, mask_ref, o_ref):
    r = rpb_ref[...]
    r1 = r.astype(BF16)
    e1 = r - r1.astype(F32)
    r2 = e1.astype(BF16)
    r3 = (e1 - r2.astype(F32)).astype(BF16)
    oh = onehot_ref[...]
    o_ref[...] = (_dot(r1, oh) + _dot(r2, oh) + _dot(r3, oh)) * LOG2E + mask_ref[...]


def _na_bias_table(rpb):
    n_ro, n_co = 2 * NA_WIN_ROWS - 1, 2 * NA_WIN_COLS - 1
    c = np.arange(GRID_W)[:, None]
    kc = np.arange(GRID_W)[None, :]
    cs = np.clip(c - NA_WIN_COLS // 2, 0, GRID_W - NA_WIN_COLS)
    valid = (kc >= cs) & (kc < cs + NA_WIN_COLS)
    col_off = kc - c + NA_WIN_COLS - 1
    onehot = (np.arange(LANES)[:, None, None] == col_off[None]) & valid[None]
    onehot = jnp.asarray(onehot.reshape(LANES, GRID_W * GRID_W), dtype=BF16)
    mask = jnp.asarray(np.where(valid, 0.0, NEG_BIG).reshape(1, GRID_W * GRID_W), dtype=F32)
    rp = jnp.pad(rpb.astype(F32).reshape(NA_HEADS * n_ro, n_co), ((0, 0), (0, LANES - n_co)))
    toep = pl.pallas_call(
        _na_bias_kernel,
        out_shape=jax.ShapeDtypeStruct((NA_HEADS * n_ro, GRID_W * GRID_W), F32),
        name="na_bias",
    )(rp, onehot, mask)
    toep = toep.reshape(NA_HEADS, n_ro, GRID_W, GRID_W).transpose(0, 2, 1, 3)
    toep = toep.reshape(NA_HEADS * GRID_W, n_ro * GRID_W)
    width = (n_ro - 1) * GRID_W
    return jnp.stack([toep[:, :width], toep[:, GRID_W:]])


def _natten(qn, kn, vn, bias_tab, rows_per_step):
    b, s, _ = qn.shape
    rows = s // GRID_W
    full = pl.BlockSpec((None, s, NA_W), lambda bi, r: (bi, 0, 0))
    tile = pl.BlockSpec((None, rows_per_step * GRID_W, NA_W), lambda bi, r: (bi, r, 0))
    return pl.pallas_call(
        functools.partial(_natten_kernel, rows=rows, rows_per_step=rows_per_step),
        grid=(b, rows // rows_per_step),
        in_specs=[tile, full, full, _const_spec(bias_tab.shape)],
        out_specs=tile,
        out_shape=jax.ShapeDtypeStruct((b, s, NA_W), BF16),
        compiler_params=pltpu.CompilerParams(dimension_semantics=("parallel", "arbitrary"),
                                             vmem_limit_bytes=VMEM_LIMIT),
        name="natten",
    )(qn, kn, vn, bias_tab)


def _merge_kernel(x_ref, hf_ref, hb_ref, hna_ref, km_ref, vm_ref,
                  g_ref, w2_ref, gml_ref, gmq_ref, gm64_ref,
                  wpm_ref, wpn_ref, wpx_ref, wout_ref, o_ref):
    x = x_ref[...]
    d = x.shape[1]
    h = _rms_rows(x, g_ref[...]).astype(BF16)
    o_pre = _dot(h, w2_ref[:, :V_W])
    q_mem = _dot(h, w2_ref[:, V_W:V_W + MEM_W])
    g0 = V_W + MEM_W

    hm = hf_ref[...] + hb_ref[...]
    gml = gml_ref[...]
    parts = []
    for hd in range(MLSTM_HEADS):
        sl = slice(hd * MLSTM_V_DIM, (hd + 1) * MLSTM_V_DIM)
        parts.append(_rms_rows(hm[:, sl], gml[:, sl]))
    hm = (jnp.concatenate(parts, axis=1) * jax.nn.sigmoid(o_pre)).astype(BF16)
    y = jax.nn.sigmoid(_dot(h, w2_ref[:, g0:g0 + d])) * _dot(hm, wpm_ref[...])

    y = y + jax.nn.sigmoid(_dot(h, w2_ref[:, g0 + d:g0 + 2 * d])) * _dot(hna_ref[...], wpn_ref[...])

    qn = q_mem * lax.rsqrt(_group_mean_sq(q_mem, gm64_ref[...]) + EPS) * gmq_ref[...]
    qn = (qn * (MEM_HEAD_DIM ** -0.5 * LOG2E)).astype(BF16)
    att = _masked_attention(_stack_heads(qn, MEM_HEADS, MEM_HEAD_DIM), km_ref[...], vm_ref[...], None)
    h_mem = _unstack_heads(att, MEM_HEADS, MEM_HEAD_DIM).astype(BF16)
    y = y + jax.nn.sigmoid(_dot(h, w2_ref[:, g0 + 2 * d:g0 + 3 * d])) * _dot(h_mem, wpx_ref[...])

    o_ref[...] = x + _dot(y.astype(BF16), wout_ref[...])


def _merge(x2, hf, hb, hna, k_mem, v_mem, g_mix, w2, g_mlstm, gmq, wpm, wpn, wpx, wout, tm, s):
    n, d = x2.shape
    m = k_mem.shape[1]
    per_b = s // tm
    row = lambda width: pl.BlockSpec((tm, width), lambda i: (i, 0))
    memblk = pl.BlockSpec((None, m, MEM_W), lambda i: (i // per_b, 0, 0))
    return pl.pallas_call(
        _merge_kernel,
        grid=(n // tm,),
        in_specs=[row(d), row(V_W), row(V_W), row(NA_W), memblk, memblk,
                  _const_spec((1, d)), _const_spec(w2.shape), _const_spec((1, V_W)),
                  _const_spec((1, MEM_W)), _const_spec((MEM_W, MEM_W)),
                  _const_spec(wpm.shape), _const_spec(wpn.shape), _const_spec(wpx.shape),
                  _const_spec(wout.shape)],
        out_specs=row(d),
        out_shape=jax.ShapeDtypeStruct((n, d), F32),
        compiler_params=pltpu.CompilerParams(dimension_semantics=("parallel",),
                                             vmem_limit_bytes=VMEM_LIMIT),
        name="merge",
    )(x2, hf, hb, hna, k_mem, v_mem, g_mix, w2, g_mlstm, gmq,
      _group_mean_matrix(MEM_W, MEM_HEAD_DIM), wpm, wpn, wpx, wout)


def _ffn_kernel(x_ref, g_ref, wu_ref, wd_ref, o_ref, *, n_chunks):
    x = x_ref[...]
    h = _rms_rows(x, g_ref[...]).astype(BF16)
    ck = wu_ref.shape[1] // n_chunks
    acc = x
    for j in range(n_chunks):
        u = jnp.maximum(_dot(h, wu_ref[:, j * ck:(j + 1) * ck]), 0.0)
        acc = acc + _dot((u * u).astype(BF16), wd_ref[j * ck:(j + 1) * ck, :])
    o_ref[...] = acc


def _ffn(x2, g_ffn, w_up, w_down, tm):
    n, d = x2.shape
    row = pl.BlockSpec((tm, d), lambda i: (i, 0))
    return pl.pallas_call(
        functools.partial(_ffn_kernel, n_chunks=w_up.shape[1] // d),
        grid=(n // tm,),
        in_specs=[row, _const_spec((1, d)), _const_spec(w_up.shape), _const_spec(w_down.shape)],
        out_specs=row,
        out_shape=jax.ShapeDtypeStruct((n, d), F32),
        compiler_params=pltpu.CompilerParams(dimension_semantics=("parallel",),
                                             vmem_limit_bytes=VMEM_LIMIT),
        name="ffn",
    )(x2, g_ffn, w_up, w_down)


def _layer(x, mem, g_mix, w_in, conv_w, conv_b, b_igate, b_fgate, g_mlstm, w_proj_mlstm,
           g_na_q, g_na_k, rpb, w_proj_na, g_mem, w_mem_kv, g_mem_q, g_mem_k,
           w_proj_mem, w_out, g_ffn, w_up, w_down):
    bsz, s, d = x.shape
    n = bsz * s
    tm = min(ROW_TILE, s)
    x2 = x.reshape(n, d)

    o_qk, o_v = 0, 2 * QK_W
    o_o = o_v + V_W
    o_i = o_o + V_W
    o_f = o_i + 2 * MLSTM_HEADS
    o_na = o_f + 2 * MLSTM_HEADS
    o_qm = o_na + 3 * NA_W
    o_g = o_qm + MEM_W
    gate_pad = ((0, 0), (0, GATE_PAD - N_GATES // 2))
    w_b = w_in.astype(BF16)
    w_gi = jnp.pad(w_b[:, o_i:o_f], gate_pad)
    w_gf = jnp.pad(w_b[:, o_f:o_na], gate_pad)
    bias_i = jnp.pad(b_igate.reshape(1, -1).astype(F32), gate_pad)
    bias_f = jnp.pad(b_fgate.reshape(1, -1).astype(F32), gate_pad)
    w1 = jnp.concatenate([w_b[:, o_qk:o_o], w_gi, w_gf, w_b[:, o_na:o_qm]], axis=1)
    w2 = jnp.concatenate([w_b[:, o_o:o_i], w_b[:, o_qm:o_g], w_b[:, o_g:]], axis=1)
    row = lambda a: a.reshape(1, -1).astype(F32)

    qk, v, gi, gf, qn, kn, vn = _proj_in(x2, row(g_mix), w1, row(jnp.tile(g_na_q, NA_HEADS)),
                                         row(jnp.tile(g_na_k, NA_HEADS)), conv_w.astype(F32),
                                         row(conv_b), tm, s)
    k_mem, v_mem = _mem_kv(mem, row(g_mem), w_mem_kv.astype(BF16), row(jnp.tile(g_mem_k, MEM_HEADS)))

    L = min(MLSTM_CHUNK, s)
    hf, hb = _mlstm(qk.reshape(bsz, s, -1), v.reshape(bsz, s, -1), gi.reshape(bsz, s, -1),
                    gf.reshape(bsz, s, -1), bias_i, bias_f, L, 2 if bsz % 2 == 0 else 1)
    hna = _natten(qn.reshape(bsz, s, -1), kn.reshape(bsz, s, -1), vn.reshape(bsz, s, -1),
                  _na_bias_table(rpb), 16 if (s // GRID_W) % 16 == 0 else 1)

    x1 = _merge(x2, hf.reshape(n, -1), hb.reshape(n, -1), hna.reshape(n, -1), k_mem, v_mem,
                row(g_mix), w2, row(g_mlstm), row(jnp.tile(g_mem_q, MEM_HEADS)),
                w_proj_mlstm.astype(BF16), w_proj_na.astype(BF16), w_proj_mem.astype(BF16),
                w_out.astype(BF16), tm, s)
    out = _ffn(x1, row(g_ffn), w_up.astype(BF16), w_down.astype(BF16), tm)
    return out.reshape(bsz, s, d)


def kernel(x, mem, g_mix, w_in, conv_w, conv_b, b_igate, b_fgate, g_mlstm, w_proj_mlstm,
           g_na_q, g_na_k, rpb, w_proj_na, g_mem, w_mem_kv, g_mem_q, g_mem_k,
           w_proj_mem, w_out, g_ffn, w_up, w_down):
    for l in range(g_mix.shape[0]):
        x = _layer(x, mem, g_mix[l], w_in[l], conv_w[l], conv_b[l], b_igate[l], b_fgate[l],
                   g_mlstm[l], w_proj_mlstm[l], g_na_q[l], g_na_k[l], rpb[l], w_proj_na[l],
                   g_mem[l], w_mem_kv[l], g_mem_q[l], g_mem_k[l], w_proj_mem[l], w_out[l],
                   g_ffn[l], w_up[l], w_down[l])
    return x
```

```python
import functools

import numpy as np
import jax
import jax.numpy as jnp
from jax import lax
from jax.experimental import pallas as pl
from jax.experimental.pallas import tpu as pltpu

GRID_W = 64
MLSTM_HEADS = 4
MLSTM_QK_DIM = 64
MLSTM_V_DIM = 128
MLSTM_CONV = 5
NA_HEADS = 8
NA_HEAD_DIM = 32
NA_WIN_ROWS = 8
NA_WIN_COLS = 16
MEM_HEADS = 4
MEM_HEAD_DIM = 64
N_BRANCH = 3
EPS = 1e-6

QK_W = MLSTM_HEADS * MLSTM_QK_DIM
V_W = MLSTM_HEADS * MLSTM_V_DIM
NA_W = NA_HEADS * NA_HEAD_DIM
MEM_W = MEM_HEADS * MEM_HEAD_DIM
N_GATES = 4 * MLSTM_HEADS
LANES = 128
SUBLANES = 8
HALO = SUBLANES
GATE_PAD = LANES
NEG_BIG = -1e30
LOG2E = 1.4426950408889634

ROW_TILE = 1024
MLSTM_CHUNK = 512
VMEM_LIMIT = 48 * 1024 * 1024

BF16 = jnp.bfloat16
F32 = jnp.float32


def _dot(a, b):
    return jnp.dot(a, b, preferred_element_type=F32)


def _dot_nt(a, b):
    return lax.dot_general(a, b, (((1,), (1,)), ((), ())), preferred_element_type=F32)


def _dot_tn(a, b):
    return lax.dot_general(a, b, (((0,), (0,)), ((), ())), preferred_element_type=F32)


def _group_mean_sq(a, m_bf16):
    return _dot((a * a).astype(BF16), m_bf16)


def _rms_rows(x, g):
    ms = jnp.mean(x * x, axis=-1, keepdims=True)
    return x * lax.rsqrt(ms + EPS) * g


def _group_mean_matrix(width, group):
    idx = np.arange(width) // group
    return jnp.asarray((idx[:, None] == idx[None, :]).astype(np.float32) / group, dtype=BF16)


def _const_spec(shape):
    nd = len(shape)
    return pl.BlockSpec(shape, lambda *_: (0,) * nd, pipeline_mode=pl.Buffered(1))


def _proj_in_kernel(x_ref, xp_ref, xn_ref, g_ref, w_ref, gm_ref, gq_ref, gk_ref, cw_ref, cb_ref,
                    qk_ref, v_ref, gi_ref, gf_ref, qn_ref, kn_ref, vn_ref, *, tiles_per_seq):
    i = pl.program_id(0)
    tm = x_ref.shape[0]
    g = g_ref[...]
    h = _rms_rows(x_ref[...], g).astype(BF16)
    pos = i % tiles_per_seq
    hp = jnp.where(pos > 0, _rms_rows(xp_ref[...], g), 0.0).astype(BF16)
    hn = jnp.where(pos < tiles_per_seq - 1, _rms_rows(xn_ref[...], g), 0.0).astype(BF16)
    o = 0
    z = _dot(jnp.concatenate([hp, h, hn], axis=0), w_ref[:, o:o + 2 * QK_W]); o += 2 * QK_W
    cw = cw_ref[...]
    acc = jnp.zeros((tm, 2 * QK_W), F32) + cb_ref[...]
    for j in range(MLSTM_CONV):
        d = j - MLSTM_CONV // 2
        tap = z if d == 0 else pltpu.roll(z, (-d) % z.shape[0], 0)
        acc = acc + tap[HALO:HALO + tm, :] * cw[j:j + 1, :]
    act = acc * jax.nn.sigmoid(acc)
    lane = lax.broadcasted_iota(jnp.int32, act.shape, 1)
    qk_ref[...] = jnp.where(lane < QK_W, act * (MLSTM_QK_DIM ** -0.5), act).astype(BF16)
    v_ref[...] = _dot(h, w_ref[:, o:o + V_W]).astype(BF16); o += V_W
    gi_ref[...] = _dot(h, w_ref[:, o:o + GATE_PAD]); o += GATE_PAD
    gf_ref[...] = _dot(h, w_ref[:, o:o + GATE_PAD]); o += GATE_PAD
    q = _dot(h, w_ref[:, o:o + NA_W]); o += NA_W
    k = _dot(h, w_ref[:, o:o + NA_W]); o += NA_W
    vn_ref[...] = _dot(h, w_ref[:, o:o + NA_W]).astype(BF16)
    gm = gm_ref[...]
    qn = q * lax.rsqrt(_group_mean_sq(q, gm) + EPS) * gq_ref[...]
    qn_ref[...] = (qn * (NA_HEAD_DIM ** -0.5 * LOG2E)).astype(BF16)
    kn_ref[...] = (k * lax.rsqrt(_group_mean_sq(k, gm) + EPS) * gk_ref[...]).astype(BF16)


def _proj_in(x2, g_mix, w1, gq, gk, conv_w, conv_b, tm, s):
    n, d = x2.shape
    w_cols = w1.shape[1]
    hb = tm // HALO
    row = lambda width: pl.BlockSpec((tm, width), lambda i: (i, 0))
    prev = pl.BlockSpec((HALO, d), lambda i: (jnp.maximum(i * hb - 1, 0), 0))
    nxt = pl.BlockSpec((HALO, d), lambda i: (jnp.minimum((i + 1) * hb, n // HALO - 1), 0))
    return pl.pallas_call(
        functools.partial(_proj_in_kernel, tiles_per_seq=s // tm),
        grid=(n // tm,),
        in_specs=[row(d), prev, nxt, _const_spec((1, d)), _const_spec((d, w_cols)),
                  _const_spec((NA_W, NA_W)), _const_spec((1, NA_W)), _const_spec((1, NA_W)),
                  _const_spec((MLSTM_CONV, 2 * QK_W)), _const_spec((1, 2 * QK_W))],
        out_specs=[row(2 * QK_W), row(V_W), row(GATE_PAD), row(GATE_PAD), row(NA_W), row(NA_W), row(NA_W)],
        out_shape=[jax.ShapeDtypeStruct((n, 2 * QK_W), BF16),
                   jax.ShapeDtypeStruct((n, V_W), BF16),
                   jax.ShapeDtypeStruct((n, GATE_PAD), F32),
                   jax.ShapeDtypeStruct((n, GATE_PAD), F32),
                   jax.ShapeDtypeStruct((n, NA_W), BF16),
                   jax.ShapeDtypeStruct((n, NA_W), BF16),
                   jax.ShapeDtypeStruct((n, NA_W), BF16)],
        compiler_params=pltpu.CompilerParams(dimension_semantics=("parallel",),
                                             vmem_limit_bytes=VMEM_LIMIT),
        name="proj_in",
    )(x2, x2, x2, g_mix, w1, _group_mean_matrix(NA_W, NA_HEAD_DIM), gq, gk, conv_w, conv_b)


def _mem_kv_kernel(mem_ref, g_ref, w_ref, gm_ref, gk_ref, k_ref, v_ref):
    h = _rms_rows(mem_ref[...], g_ref[...]).astype(BF16)
    k = _dot(h, w_ref[:, :MEM_W])
    v_ref[...] = _dot(h, w_ref[:, MEM_W:]).astype(BF16)
    k_ref[...] = (k * lax.rsqrt(_group_mean_sq(k, gm_ref[...]) + EPS) * gk_ref[...]).astype(BF16)


def _mem_kv(mem, g_mem, w_kv, gk):
    b, m, d = mem.shape
    blk = lambda width: pl.BlockSpec((None, m, width), lambda i: (i, 0, 0))
    return pl.pallas_call(
        _mem_kv_kernel,
        grid=(b,),
        in_specs=[blk(d), _const_spec((1, d)), _const_spec((d, 2 * MEM_W)),
                  _const_spec((MEM_W, MEM_W)), _const_spec((1, MEM_W))],
        out_specs=[blk(MEM_W), blk(MEM_W)],
        out_shape=[jax.ShapeDtypeStruct((b, m, MEM_W), BF16)] * 2,
        compiler_params=pltpu.CompilerParams(dimension_semantics=("parallel",),
                                             vmem_limit_bytes=VMEM_LIMIT),
        name="mem_kv",
    )(mem, g_mem, w_kv, _group_mean_matrix(MEM_W, MEM_HEAD_DIM), gk)


def _log_sigmoid(x):
    return jnp.minimum(x, 0.0) - jnp.log1p(jnp.exp(-jnp.abs(x)))


def _cummax_rows(x, reverse):
    n = x.shape[0]
    row = lax.broadcasted_iota(jnp.int32, x.shape, 0)
    sh = 1
    while sh < n:
        if reverse:
            shifted, ok = pltpu.roll(x, n - sh, 0), row < n - sh
        else:
            shifted, ok = pltpu.roll(x, sh, 0), row >= sh
        x = jnp.where(ok, jnp.maximum(x, shifted), x)
        sh *= 2
    return x


def _mlstm_gates(reverse, ig, fpre, m_st):
    L = ig.shape[0]
    last = 0 if reverse else L - 1
    r_i = lax.broadcasted_iota(jnp.int32, (L, L), 0)
    c_i = lax.broadcasted_iota(jnp.int32, (L, L), 1)
    causal = (c_i >= r_i) if reverse else (c_i <= r_i)
    lf = _log_sigmoid(fpre)
    l1 = lf.astype(BF16)
    l2 = (lf - l1.astype(F32)).astype(BF16)
    bb = _dot(jnp.where(causal, 1.0, 0.0).astype(BF16), jnp.concatenate([l1, l2], axis=1))
    b = bb[:, :LANES] + bb[:, LANES:]
    a = ig - b
    big_m = jnp.maximum(m_st, _cummax_rows(a, reverse))
    a2 = a * LOG2E
    m2 = big_m * LOG2E
    return dict(
        causal=causal, last=last,
        wint=jnp.exp(m_st - big_m),
        eneg=jnp.exp(-(b + big_m)),
        m2=m2,
        a2_rows=a2.T,
        ws=jnp.exp2(a2 - m2[last:last + 1, :]),
        m_next=(b + big_m)[last:last + 1, :],
    )


def _mlstm_chains(d, g, qk, v, states):
    L = qk.shape[0]
    half = L // 2
    H = MLSTM_HEADS
    causal, last = g["causal"], g["last"]
    lane = lax.broadcasted_iota(jnp.int32, (L, LANES), 1)
    nums, new_states = [], []
    den_all = jnp.zeros((L, LANES), F32)
    for hd in range(H):
        c = d * H + hd
        pair = hd // 2
        in_head = (lane // MLSTM_QK_DIM) == (hd % 2)
        q_h = jnp.where(in_head, qk[:, pair * LANES:(pair + 1) * LANES], jnp.zeros((), BF16))
        k_p = qk[:, QK_W + pair * LANES:QK_W + (pair + 1) * LANES]
        v_ext = jnp.concatenate([v[:, hd * LANES:(hd + 1) * LANES],
                                 jnp.where(lane == c, 1.0, 0.0).astype(BF16)], axis=1)
        st = states[hd]
        st_b = st.astype(BF16)
        a2_row, m2_col = g["a2_rows"][c:c + 1, :], g["m2"][:, c:c + 1]
        q_inter = (q_h.astype(F32) * g["wint"][:, c:c + 1]).astype(BF16)

        def block(rows, keys):
            decay_log = a2_row[:, keys] - m2_col[rows, :]
            p = jnp.exp2(jnp.where(causal[rows, keys], decay_log, -jnp.inf))
            s_mat = _dot_nt(q_h[rows, :], k_p[keys, :]) * p
            lhs = jnp.concatenate([s_mat.astype(BF16), q_inter[rows, :]], axis=1)
            return _dot(lhs, jnp.concatenate([v_ext[keys, :], st_b], axis=0))

        first, second = (slice(half, L), slice(0, half)) if g["last"] == 0 else (slice(0, half), slice(half, L))
        num_first, num_second = block(first, first), block(second, slice(0, L))
        num_ext = jnp.concatenate([num_second, num_first] if g["last"] == 0 else [num_first, num_second], axis=0)
        nums.append(num_ext[:, :LANES])
        den_all = den_all + num_ext[:, LANES:]
        kw = (k_p.astype(F32) * g["ws"][:, c:c + 1]).astype(BF16)
        new_states.append(g["wint"][last:last + 1, c:c + 1] * st + _dot_tn(kw, v_ext))
    r_all = 1.0 / jnp.maximum(jnp.abs(den_all), g["eneg"])
    h = jnp.concatenate([nums[hd] * r_all[:, d * H + hd:d * H + hd + 1] for hd in range(H)], axis=1)
    return h, new_states


def _mlstm_kernel(qkf_ref, qkb_ref, vf_ref, vb_ref, gif_ref, gib_ref, gff_ref, gfb_ref, bi_ref, bf_ref,
                  hf_ref, hb_ref, st_ref, m_ref):
    H = MLSTM_HEADS

    @pl.when(pl.program_id(1) == 0)
    def _():
        st_ref[...] = jnp.zeros_like(st_ref)
        m_ref[...] = jnp.full(m_ref.shape, -jnp.inf, F32)

    bi, bf = bi_ref[...], bf_ref[...]
    results = []
    for e in range(qkf_ref.shape[0]):
        g_f = _mlstm_gates(False, gif_ref[e] + bi, gff_ref[e] + bf, m_ref[2 * e, 0:1, :])
        g_b = _mlstm_gates(True, gib_ref[e] + bi, gfb_ref[e] + bf, m_ref[2 * e + 1, 0:1, :])
        base = 2 * H * e
        h_f, st_f = _mlstm_chains(0, g_f, qkf_ref[e], vf_ref[e], [st_ref[base + c] for c in range(H)])
        h_b, st_b = _mlstm_chains(1, g_b, qkb_ref[e], vb_ref[e], [st_ref[base + H + c] for c in range(H)])
        results.append((h_f, h_b, st_f + st_b, g_f["m_next"], g_b["m_next"]))
    for e, (h_f, h_b, sts, m_f, m_b) in enumerate(results):
        hf_ref[e] = h_f
        hb_ref[e] = h_b
        for c, st in enumerate(sts):
            st_ref[2 * H * e + c] = st
        m_ref[2 * e] = jnp.broadcast_to(m_f, m_ref.shape[1:])
        m_ref[2 * e + 1] = jnp.broadcast_to(m_b, m_ref.shape[1:])


def _mlstm(qk3, v3, gi3, gf3, bias_i, bias_f, L, nb):
    b, s, _ = qk3.shape
    nc = s // L
    fwd = lambda bi, c: (bi, c, 0)
    bwd = lambda bi, c: (bi, nc - 1 - c, 0)
    blk = lambda width, pos: pl.BlockSpec((nb, L, width), pos)
    return pl.pallas_call(
        _mlstm_kernel,
        grid=(b // nb, nc),
        in_specs=[blk(2 * QK_W, fwd), blk(2 * QK_W, bwd), blk(V_W, fwd), blk(V_W, bwd),
                  blk(GATE_PAD, fwd), blk(GATE_PAD, bwd), blk(GATE_PAD, fwd), blk(GATE_PAD, bwd),
                  _const_spec((1, GATE_PAD)), _const_spec((1, GATE_PAD))],
        out_specs=[blk(V_W, fwd), blk(V_W, bwd)],
        out_shape=[jax.ShapeDtypeStruct((b, s, V_W), F32)] * 2,
        scratch_shapes=[pltpu.VMEM((nb * 2 * MLSTM_HEADS, LANES, 2 * LANES), F32),
                        pltpu.VMEM((nb * 2, SUBLANES, LANES), F32)],
        compiler_params=pltpu.CompilerParams(dimension_semantics=("parallel", "arbitrary"),
                                             vmem_limit_bytes=VMEM_LIMIT),
        name="mlstm",
    )(qk3, qk3, v3, v3, gi3, gi3, gf3, gf3, bias_i, bias_f)


def _stack_heads(q, n_heads, head_dim):
    lane = lax.broadcasted_iota(jnp.int32, q.shape, 1)
    zero = jnp.zeros_like(q)
    return jnp.concatenate([jnp.where(lane // head_dim == h, q, zero) for h in range(n_heads)], axis=0)


def _unstack_heads(o, n_heads, head_dim):
    t = o.shape[0] // n_heads
    lane = lax.broadcasted_iota(jnp.int32, (t, o.shape[1]), 1)
    acc = jnp.zeros((t, o.shape[1]), o.dtype)
    for h in range(n_heads):
        acc = jnp.where(lane // head_dim == h, o[h * t:(h + 1) * t, :], acc)
    return acc


def _masked_attention(q_stacked, k, v, bias):
    sc = _dot_nt(q_stacked, k)
    if bias is not None:
        sc = sc + bias
    p = jnp.exp2(sc - jnp.max(sc, axis=-1, keepdims=True))
    o = _dot(p.astype(BF16), v)
    return o / jnp.sum(p, axis=-1, keepdims=True)


def _natten_kernel(q_ref, k_ref, v_ref, bias_ref, o_ref, *, rows, rows_per_step):
    for j in range(rows_per_step):
        r = pl.program_id(1) * rows_per_step + j
        rs = jnp.clip(r - NA_WIN_ROWS // 2, 0, rows - NA_WIN_ROWS)
        start = pl.multiple_of(rs * GRID_W, GRID_W)
        k_win = k_ref[pl.ds(start, NA_WIN_ROWS * GRID_W), :]
        v_win = v_ref[pl.ds(start, NA_WIN_ROWS * GRID_W), :]
        q = q_ref[j * GRID_W:(j + 1) * GRID_W, :]
        off = NA_WIN_ROWS - 1 - (r - rs)
        lane0 = pl.multiple_of((off // 2) * LANES, LANES)
        bias = bias_ref[off % 2, :, pl.ds(lane0, NA_WIN_ROWS * GRID_W)]
        o = _masked_attention(_stack_heads(q, NA_HEADS, NA_HEAD_DIM), k_win, v_win, bias)
        o_ref[j * GRID_W:(j + 1) * GRID_W, :] = _unstack_heads(o, NA_HEADS, NA_HEAD_DIM).astype(o_ref.dtype)


def _na_bias_kernel(rpb_ref, onehot_ref, mask_ref, o_ref):
    r = rpb_ref[...]
    r1 = r.astype(BF16)
    e1 = r - r1.astype(F32)
    r2 = e1.astype(BF16)
    r3 = (e1 - r2.astype(F32)).astype(BF16)
    oh = onehot_ref[...]
    o_ref[...] = (_dot(r1, oh) + _dot(r2, oh) + _dot(r3, oh)) * LOG2E + mask_ref[...]


def _na_bias_table(rpb):
    n_ro, n_co = 2 * NA_WIN_ROWS - 1, 2 * NA_WIN_COLS - 1
    c = np.arange(GRID_W)[:, None]
    kc = np.arange(GRID_W)[None, :]
    cs = np.clip(c - NA_WIN_COLS // 2, 0, GRID_W - NA_WIN_COLS)
    valid = (kc >= cs) & (kc < cs + NA_WIN_COLS)
    col_off = kc - c + NA_WIN_COLS - 1
    onehot = (np.arange(LANES)[:, None, None] == col_off[None]) & valid[None]
    onehot = jnp.asarray(onehot.reshape(LANES, GRID_W * GRID_W), dtype=BF16)
    mask = jnp.asarray(np.where(valid, 0.0, NEG_BIG).reshape(1, GRID_W * GRID_W), dtype=F32)
    rp = jnp.pad(rpb.astype(F32).reshape(NA_HEADS * n_ro, n_co), ((0, 0), (0, LANES - n_co)))
    toep = pl.pallas_call(
        _na_bias_kernel,
        out_shape=jax.ShapeDtypeStruct((NA_HEADS * n_ro, GRID_W * GRID_W), F32),
        name="na_bias",
    )(rp, onehot, mask)
    toep = toep.reshape(NA_HEADS, n_ro, GRID_W, GRID_W).transpose(0, 2, 1, 3)
    toep = toep.reshape(NA_HEADS * GRID_W, n_ro * GRID_W)
    width = (n_ro - 1) * GRID_W
    return jnp.stack([toep[:, :width], toep[:, GRID_W:]])


def _natten(qn, kn, vn, bias_tab, rows_per_step):
    b, s, _ = qn.shape
    rows = s // GRID_W
    full = pl.BlockSpec((None, s, NA_W), lambda bi, r: (bi, 0, 0))
    tile = pl.BlockSpec((None, rows_per_step * GRID_W, NA_W), lambda bi, r: (bi, r, 0))
    return pl.pallas_call(
        functools.partial(_natten_kernel, rows=rows, rows_per_step=rows_per_step),
        grid=(b, rows // rows_per_step),
        in_specs=[tile, full, full, _const_spec(bias_tab.shape)],
        out_specs=tile,
        out_shape=jax.ShapeDtypeStruct((b, s, NA_W), BF16),
        compiler_params=pltpu.CompilerParams(dimension_semantics=("parallel", "arbitrary"),
                                             vmem_limit_bytes=VMEM_LIMIT),
        name="natten",
    )(qn, kn, vn, bias_tab)


def _merge_kernel(x_ref, hf_ref, hb_ref, hna_ref, km_ref, vm_ref,
                  g_ref, w2_ref, gml_ref, gmq_ref, gm64_ref,
                  wpm_ref, wpn_ref, wpx_ref, wout_ref, o_ref):
    x = x_ref[...]
    d = x.shape[1]
    h = _rms_rows(x, g_ref[...]).astype(BF16)
    o_pre = _dot(h, w2_ref[:, :V_W])
    q_mem = _dot(h, w2_ref[:, V_W:V_W + MEM_W])
    g0 = V_W + MEM_W

    hm = hf_ref[...] + hb_ref[...]
    gml = gml_ref[...]
    parts = []
    for hd in range(MLSTM_HEADS):
        sl = slice(hd * MLSTM_V_DIM, (hd + 1) * MLSTM_V_DIM)
        parts.append(_rms_rows(hm[:, sl], gml[:, sl]))
    hm = (jnp.concatenate(parts, axis=1) * jax.nn.sigmoid(o_pre)).astype(BF16)
    y = jax.nn.sigmoid(_dot(h, w2_ref[:, g0:g0 + d])) * _dot(hm, wpm_ref[...])

    y = y + jax.nn.sigmoid(_dot(h, w2_ref[:, g0 + d:g0 + 2 * d])) * _dot(hna_ref[...], wpn_ref[...])

    qn = q_mem * lax.rsqrt(_group_mean_sq(q_mem, gm64_ref[...]) + EPS) * gmq_ref[...]
    qn = (qn * (MEM_HEAD_DIM ** -0.5 * LOG2E)).astype(BF16)
    att = _masked_attention(_stack_heads(qn, MEM_HEADS, MEM_HEAD_DIM), km_ref[...], vm_ref[...], None)
    h_mem = _unstack_heads(att, MEM_HEADS, MEM_HEAD_DIM).astype(BF16)
    y = y + jax.nn.sigmoid(_dot(h, w2_ref[:, g0 + 2 * d:g0 + 3 * d])) * _dot(h_mem, wpx_ref[...])

    o_ref[...] = x + _dot(y.astype(BF16), wout_ref[...])


def _merge(x2, hf, hb, hna, k_mem, v_mem, g_mix, w2, g_mlstm, gmq, wpm, wpn, wpx, wout, tm, s):
    n, d = x2.shape
    m = k_mem.shape[1]
    per_b = s // tm
    row = lambda width: pl.BlockSpec((tm, width), lambda i: (i, 0))
    memblk = pl.BlockSpec((None, m, MEM_W), lambda i: (i // per_b, 0, 0))
    return pl.pallas_call(
        _merge_kernel,
        grid=(n // tm,),
        in_specs=[row(d), row(V_W), row(V_W), row(NA_W), memblk, memblk,
                  _const_spec((1, d)), _const_spec(w2.shape), _const_spec((1, V_W)),
                  _const_spec((1, MEM_W)), _const_spec((MEM_W, MEM_W)),
                  _const_spec(wpm.shape), _const_spec(wpn.shape), _const_spec(wpx.shape),
                  _const_spec(wout.shape)],
        out_specs=row(d),
        out_shape=jax.ShapeDtypeStruct((n, d), F32),
        compiler_params=pltpu.CompilerParams(dimension_semantics=("parallel",),
                                             vmem_limit_bytes=VMEM_LIMIT),
        name="merge",
    )(x2, hf, hb, hna, k_mem, v_mem, g_mix, w2, g_mlstm, gmq,
      _group_mean_matrix(MEM_W, MEM_HEAD_DIM), wpm, wpn, wpx, wout)


def _ffn_kernel(x_ref, g_ref, wu_ref, wd_ref, o_ref, *, n_chunks):
    x = x_ref[...]
    h = _rms_rows(x, g_ref[...]).astype(BF16)
    ck = wu_ref.shape[1] // n_chunks
    acc = x
    for j in range(n_chunks):
        u = jnp.maximum(_dot(h, wu_ref[:, j * ck:(j + 1) * ck]), 0.0)
        acc = acc + _dot((u * u).astype(BF16), wd_ref[j * ck:(j + 1) * ck, :])
    o_ref[...] = acc


def _ffn(x2, g_ffn, w_up, w_down, tm):
    n, d = x2.shape
    row = pl.BlockSpec((tm, d), lambda i: (i, 0))
    return pl.pallas_call(
        functools.partial(_ffn_kernel, n_chunks=w_up.shape[1] // d),
        grid=(n // tm,),
        in_specs=[row, _const_spec((1, d)), _const_spec(w_up.shape), _const_spec(w_down.shape)],
        out_specs=row,
        out_shape=jax.ShapeDtypeStruct((n, d), F32),
        compiler_params=pltpu.CompilerParams(dimension_semantics=("parallel",),
                                             vmem_limit_bytes=VMEM_LIMIT),
        name="ffn",
    )(x2, g_ffn, w_up, w_down)


def _layer(x, mem, g_mix, w_in, conv_w, conv_b, b_igate, b_fgate, g_mlstm, w_proj_mlstm,
           g_na_q, g_na_k, rpb, w_proj_na, g_mem, w_mem_kv, g_mem_q, g_mem_k,
           w_proj_mem, w_out, g_ffn, w_up, w_down):
    bsz, s, d = x.shape
    n = bsz * s
    tm = min(ROW_TILE, s)
    x2 = x.reshape(n, d)

    o_qk, o_v = 0, 2 * QK_W
    o_o = o_v + V_W
    o_i = o_o + V_W
    o_f = o_i + 2 * MLSTM_HEADS
    o_na = o_f + 2 * MLSTM_HEADS
    o_qm = o_na + 3 * NA_W
    o_g = o_qm + MEM_W
    gate_pad = ((0, 0), (0, GATE_PAD - N_GATES // 2))
    w_b = w_in.astype(BF16)
    w_gi = jnp.pad(w_b[:, o_i:o_f], gate_pad)
    w_gf = jnp.pad(w_b[:, o_f:o_na], gate_pad)
    bias_i = jnp.pad(b_igate.reshape(1, -1).astype(F32), gate_pad)
    bias_f = jnp.pad(b_fgate.reshape(1, -1).astype(F32), gate_pad)
    w1 = jnp.concatenate([w_b[:, o_qk:o_o], w_gi, w_gf, w_b[:, o_na:o_qm]], axis=1)
    w2 = jnp.concatenate([w_b[:, o_o:o_i], w_b[:, o_qm:o_g], w_b[:, o_g:]], axis=1)
    row = lambda a: a.reshape(1, -1).astype(F32)

    qk, v, gi, gf, qn, kn, vn = _proj_in(x2, row(g_mix), w1, row(jnp.tile(g_na_q, NA_HEADS)),
                                         row(jnp.tile(g_na_k, NA_HEADS)), conv_w.astype(F32),
                                         row(conv_b), tm, s)
    k_mem, v_mem = _mem_kv(mem, row(g_mem), w_mem_kv.astype(BF16), row(jnp.tile(g_mem_k, MEM_HEADS)))

    L = min(MLSTM_CHUNK, s)
    hf, hb = _mlstm(qk.reshape(bsz, s, -1), v.reshape(bsz, s, -1), gi.reshape(bsz, s, -1),
                    gf.reshape(bsz, s, -1), bias_i, bias_f, L, 2 if bsz % 2 == 0 else 1)
    hna = _natten(qn.reshape(bsz, s, -1), kn.reshape(bsz, s, -1), vn.reshape(bsz, s, -1),
                  _na_bias_table(rpb), 16 if (s // GRID_W) % 16 == 0 else 1)

    x1 = _merge(x2, hf.reshape(n, -1), hb.reshape(n, -1), hna.reshape(n, -1), k_mem, v_mem,
                row(g_mix), w2, row(g_mlstm), row(jnp.tile(g_mem_q, MEM_HEADS)),
                w_proj_mlstm.astype(BF16), w_proj_na.astype(BF16), w_proj_mem.astype(BF16),
                w_out.astype(BF16), tm, s)
    out = _ffn(x1, row(g_ffn), w_up.astype(BF16), w_down.astype(BF16), tm)
    return out.reshape(bsz, s, d)


def kernel(x, mem, g_mix, w_in, conv_w, conv_b, b_igate, b_fgate, g_mlstm, w_proj_mlstm,
           g_na_q, g_na_k, rpb, w_proj_na, g_mem, w_mem_kv, g_mem_q, g_mem_k,
           w_proj_mem, w_out, g_ffn, w_up, w_down):
    for l in range(g_mix.shape[0]):
        x = _layer(x, mem, g_mix[l], w_in[l], conv_w[l], conv_b[l], b_igate[l], b_fgate[l],
                   g_mlstm[l], w_proj_mlstm[l], g_na_q[l], g_na_k[l], rpb[l], w_proj_na[l],
                   g_mem[l], w_mem_kv[l], g_mem_q[l], g_mem_k[l], w_proj_mem[l], w_out[l],
                   g_ffn[l], w_up[l], w_down[l])
    return x
```
